```python
import math
import jax, jax.numpy as jnp
from jax import lax
import numpy as np

D_MODEL = 4096
BATCH = 4
SEQ = 2048
DEPTH = 2
DEC_BATCH = 128
DEC_SEQ = 8
PAST_LEN = 16384
PAGE_SIZE = 128

N_MIXERS = 2
N_GLA = (DEPTH + 1) // 2
N_CONV = DEPTH // 2
GLA_HEADS = 4
GLA_DK = D_MODEL // 2
GLA_DV = D_MODEL
GLA_DK_HEAD = GLA_DK // GLA_HEADS
GLA_DV_HEAD = GLA_DV // GLA_HEADS
GLA_GATE_RANK = 16
GLA_GATE_NORM = 16.0
GLA_CHUNK = 64
CONV_CH = D_MODEL
CONV_WIDTH = 31
MOE_GROUPS = 8
MOE_PER_GROUP = 8
MOE_EXPERTS = MOE_GROUPS * MOE_PER_GROUP
MOE_TOPK = 2
MOE_FF = D_MODEL // 4
MOE_BLOCK = 128
EPS = 1e-6

kernel_name = "gla_conformer_conv_hier_moe_step"


def rmsnorm(x, g):
    xf = x.astype(jnp.float32)
    y = xf * lax.rsqrt(jnp.mean(xf * xf, axis=-1, keepdims=True) + EPS)
    return (y * g.astype(jnp.float32)).astype(x.dtype)


def layernorm(x, g, b):
    xf = x.astype(jnp.float32)
    mu = jnp.mean(xf, axis=-1, keepdims=True)
    var = jnp.mean(jnp.square(xf - mu), axis=-1, keepdims=True)
    y = (xf - mu) * lax.rsqrt(var + EPS)
    return (y * g.astype(jnp.float32) + b.astype(jnp.float32)).astype(x.dtype)


def gla_scan(q, k, v, logf, s0):
    B, L, H, DK = q.shape
    DV = v.shape[-1]
    C = math.gcd(L, GLA_CHUNK)
    n = L // C

    def blocks(t):
        return t.reshape(B, n, C, H, t.shape[-1]).transpose(1, 0, 3, 2, 4).astype(jnp.float32)

    mask = jnp.tril(jnp.ones((C, C), dtype=bool))

    def step(S, inp):
        qc, kc, vc, gc = inp
        b = jnp.cumsum(gc, axis=2)
        b_last = b[:, :, -1:, :]
        qe = qc * jnp.exp(b)
        ke = kc * jnp.exp(-b)
        scores = jnp.where(mask, jnp.einsum("bhck,bhsk->bhcs", qe, ke), 0.0)
        o = jnp.einsum("bhck,bhkv->bhcv", qe, S) + jnp.einsum("bhcs,bhsv->bhcv", scores, vc)
        S_new = jnp.exp(b_last)[:, :, 0, :, None] * S + jnp.einsum(
            "bhsk,bhsv->bhkv", kc * jnp.exp(b_last - b), vc)
        return S_new, o

    S_fin, o = lax.scan(step, s0.astype(jnp.float32), (blocks(q), blocks(k), blocks(v), blocks(logf)))
    o = o.transpose(1, 0, 3, 2, 4).reshape(B, L, H, DV)
    return o, S_fin


def gla_mixer(h, s0, w_in, w_g2, b_g2, norm_g, w_o):
    B, L, _ = h.shape
    proj = h @ w_in
    q, k, v, r, gl = jnp.split(
        proj, [GLA_DK, 2 * GLA_DK, 2 * GLA_DK + GLA_DV, 2 * GLA_DK + 2 * GLA_DV], axis=-1)
    logf = jax.nn.log_sigmoid((gl @ w_g2 + b_g2).astype(jnp.float32)) / GLA_GATE_NORM
    q = q.reshape(B, L, GLA_HEADS, GLA_DK_HEAD) * (GLA_DK_HEAD ** -0.5)
    k = k.reshape(B, L, GLA_HEADS, GLA_DK_HEAD)
    v = v.reshape(B, L, GLA_HEADS, GLA_DV_HEAD)
    logf = logf.reshape(B, L, GLA_HEADS, GLA_DK_HEAD)
    o, S = gla_scan(q, k, v, logf, s0)
    o = rmsnorm(o, norm_g).reshape(B, L, GLA_DV) * jax.nn.silu(r.astype(jnp.float32))
    y = o.astype(h.dtype) @ w_o
    return y, S.astype(s0.dtype)


def conv_mixer(h, buf, w_pw1, b_pw1, w_dw, b_dw, ln_g, ln_b, w_pw2, b_pw2):
    u = h @ w_pw1 + b_pw1
    a, gate = jnp.split(u, 2, axis=-1)
    u = a * jax.nn.sigmoid(gate)
    full = jnp.concatenate([buf.astype(u.dtype), u], axis=1)
    z = lax.conv_general_dilated(
        full, w_dw[:, None, :].astype(u.dtype), window_strides=(1,), padding="VALID",
        dimension_numbers=("NWC", "WIO", "NWC"), feature_group_count=CONV_CH) + b_dw
    z = jax.nn.silu(layernorm(z, ln_g, ln_b))
    y = z @ w_pw2 + b_pw2
    return y, full[:, -(CONV_WIDTH - 1):]


def hier_moe(x2d, w_grp, b_grp, w_exp, b_exp, w_gate, w_up, w_down):
    T, D = x2d.shape
    xf = x2d.astype(jnp.float32)
    grp_logits = xf @ w_grp.astype(jnp.float32) + b_grp.astype(jnp.float32)
    grp_prob = jax.nn.softmax(grp_logits, axis=-1)
    g_sel = jnp.argmax(grp_logits, axis=-1).astype(jnp.int32)
    p_g = jnp.take_along_axis(grp_prob, g_sel[:, None], axis=-1)
    exp_logits = (xf @ w_exp.astype(jnp.float32) + b_exp.astype(jnp.float32)).reshape(
        T, MOE_GROUPS, MOE_PER_GROUP)
    sel_logits = jnp.take_along_axis(exp_logits, g_sel[:, None, None], axis=1)[:, 0]
    top_val, top_j = lax.top_k(sel_logits, MOE_TOPK)
    gates = p_g * jax.nn.softmax(top_val, axis=-1)
    expert_ids = g_sel[:, None] * MOE_PER_GROUP + top_j.astype(jnp.int32)

    A = T * MOE_TOPK
    flat_e = expert_ids.reshape(-1)
    flat_tok = jnp.repeat(jnp.arange(T, dtype=jnp.int32), MOE_TOPK)
    flat_w = gates.reshape(-1)
    order = jnp.argsort(flat_e)
    sorted_e = flat_e[order]
    counts = jnp.bincount(flat_e, length=MOE_EXPERTS).astype(jnp.int32)
    seg_start = jnp.cumsum(counts) - counts
    padded = (counts + MOE_BLOCK - 1) // MOE_BLOCK * MOE_BLOCK
    pad_end = jnp.cumsum(padded)
    pad_start = pad_end - padded
    dest = pad_start[sorted_e] + (jnp.arange(A, dtype=jnp.int32) - seg_start[sorted_e])
    nb = -(-(A + MOE_EXPERTS * (MOE_BLOCK - 1)) // MOE_BLOCK)
    buf_tok = jnp.full((nb * MOE_BLOCK,), T, jnp.int32).at[dest].set(flat_tok[order])
    buf_w = jnp.zeros((nb * MOE_BLOCK,), jnp.float32).at[dest].set(flat_w[order])
    blk_e = jnp.minimum(
        jnp.searchsorted(pad_end, jnp.arange(nb, dtype=jnp.int32) * MOE_BLOCK, side="right"),
        MOE_EXPERTS - 1).astype(jnp.int32)
    x_pad = jnp.concatenate([x2d, jnp.zeros((1, D), x2d.dtype)], axis=0)

    def expert_block(args):
        tok, wt, e = args
        xb = x_pad[tok]
        hb = jax.nn.silu(xb @ w_gate[e]) * (xb @ w_up[e])
        return (hb @ w_down[e]) * wt[:, None].astype(x2d.dtype)

    yb = lax.map(expert_block, (buf_tok.reshape(nb, MOE_BLOCK), buf_w.reshape(nb, MOE_BLOCK), blk_e))
    return jax.ops.segment_sum(yb.reshape(-1, D), buf_tok, num_segments=T + 1)[:T]


def setup_inputs(seed: int = 0) -> dict:
    key = jax.random.key(seed)
    ks = jax.random.split(key, 32)

    def nrm(k, shape, scale):
        return jax.random.normal(k, shape, jnp.float32) * scale

    D, C, F, E = D_MODEL, CONV_CH, MOE_FF, MOE_EXPERTS
    w_in_cols = 2 * GLA_DK + 2 * GLA_DV + GLA_GATE_RANK
    return {
        "x_prompt": nrm(ks[0], (BATCH, SEQ, D), 1.0),
        "x_sample": nrm(ks[1], (DEC_BATCH, DEC_SEQ, D), 1.0),
        "state_gla": nrm(ks[2], (N_GLA, DEC_BATCH, GLA_HEADS, GLA_DK_HEAD, GLA_DV_HEAD), 1.0),
        "cache_conv": nrm(ks[3], (N_CONV, DEC_BATCH, CONV_WIDTH - 1, C), 0.5),
        "norm_mix": 1.0 + nrm(ks[4], (DEPTH, D), 0.02),
        "norm_ffn": 1.0 + nrm(ks[5], (DEPTH, D), 0.02),
        "norm_final": 1.0 + nrm(ks[6], (D,), 0.02),
        "gla_w_in": nrm(ks[7], (N_GLA, D, w_in_cols), D ** -0.5),
        "gla_w_g2": nrm(ks[8], (N_GLA, GLA_GATE_RANK, GLA_DK), GLA_GATE_RANK ** -0.5),
        "gla_b_g2": nrm(ks[9], (N_GLA, GLA_DK), 0.1),
        "gla_norm": 1.0 + nrm(ks[10], (N_GLA, GLA_DV_HEAD), 0.02),
        "gla_w_o": nrm(ks[11], (N_GLA, GLA_DV, D), GLA_DV ** -0.5),
        "conv_w_pw1": nrm(ks[12], (N_CONV, D, 2 * C), D ** -0.5),
        "conv_b_pw1": nrm(ks[13], (N_CONV, 2 * C), 0.02),
        "conv_w_dw": nrm(ks[14], (N_CONV, CONV_WIDTH, C), CONV_WIDTH ** -0.5),
        "conv_b_dw": nrm(ks[15], (N_CONV, C), 0.02),
        "conv_ln_g": 1.0 + nrm(ks[16], (N_CONV, C), 0.02),
        "conv_ln_b": nrm(ks[17], (N_CONV, C), 0.02),
        "conv_w_pw2": nrm(ks[18], (N_CONV, C, D), C ** -0.5),
        "conv_b_pw2": nrm(ks[19], (N_CONV, D), 0.02),
        "moe_w_grp": nrm(ks[20], (DEPTH, D, MOE_GROUPS), D ** -0.5),
        "moe_b_grp": nrm(ks[21], (DEPTH, MOE_GROUPS), 0.01),
        "moe_w_exp": nrm(ks[22], (DEPTH, D, E), D ** -0.5),
        "moe_b_exp": nrm(ks[23], (DEPTH, E), 0.01),
        "moe_w_gate": nrm(ks[24], (DEPTH, E, D, F), D ** -0.5),
        "moe_w_up": nrm(ks[25], (DEPTH, E, D, F), D ** -0.5),
        "moe_w_down": nrm(ks[26], (DEPTH, E, F, D), F ** -0.5),
    }


def reference(x_prompt, x_sample, state_gla, cache_conv, norm_mix, norm_ffn, norm_final,
              gla_w_in, gla_w_g2, gla_b_g2, gla_norm, gla_w_o,
              conv_w_pw1, conv_b_pw1, conv_w_dw, conv_b_dw, conv_ln_g, conv_ln_b, conv_w_pw2, conv_b_pw2,
              moe_w_grp, moe_b_grp, moe_w_exp, moe_b_exp, moe_w_gate, moe_w_up, moe_w_down):
    Bp, Lp, D = x_prompt.shape
    Bs, Ls, _ = x_sample.shape
    xp, xs = x_prompt, x_sample
    gla_p, gla_s, conv_p, conv_s = [], [], [], []
    for i in range(DEPTH):
        j = i // N_MIXERS
        hp = rmsnorm(xp, norm_mix[i])
        hs = rmsnorm(xs, norm_mix[i])
        if i % N_MIXERS == 0:
            gw = (gla_w_in[j], gla_w_g2[j], gla_b_g2[j], gla_norm[j], gla_w_o[j])
            s0p = jnp.zeros((Bp, GLA_HEADS, GLA_DK_HEAD, GLA_DV_HEAD), state_gla.dtype)
            yp, sp = gla_mixer(hp, s0p, *gw)
            ys, ss = gla_mixer(hs, state_gla[j], *gw)
            gla_p.append(sp)
            gla_s.append(ss)
        else:
            cw = (conv_w_pw1[j], conv_b_pw1[j], conv_w_dw[j], conv_b_dw[j], conv_ln_g[j],
                  conv_ln_b[j], conv_w_pw2[j], conv_b_pw2[j])
            b0p = jnp.zeros((Bp, CONV_WIDTH - 1, CONV_CH), cache_conv.dtype)
            yp, bp = conv_mixer(hp, b0p, *cw)
            ys, bs = conv_mixer(hs, cache_conv[j], *cw)
            conv_p.append(bp.astype(cache_conv.dtype))
            conv_s.append(bs.astype(cache_conv.dtype))
        xp = xp + yp
        xs = xs + ys
        tok = jnp.concatenate([xp.reshape(-1, D), xs.reshape(-1, D)], axis=0)
        f = hier_moe(rmsnorm(tok, norm_ffn[i]), moe_w_grp[i], moe_b_grp[i], moe_w_exp[i], moe_b_exp[i],
                     moe_w_gate[i], moe_w_up[i], moe_w_down[i])
        xp = xp + f[:Bp * Lp].reshape(Bp, Lp, D)
        xs = xs + f[Bp * Lp:].reshape(Bs, Ls, D)
    y_prompt = rmsnorm(xp, norm_final)
    y_sample = rmsnorm(xs, norm_final)
    state_gla_prompt = jnp.stack(gla_p)
    state_gla_sample = jnp.stack(gla_s)
    cache_conv_prompt = jnp.stack(conv_p)
    cache_conv_sample = jnp.stack(conv_s)
    return (y_prompt, y_sample, state_gla_prompt, state_gla_sample, cache_conv_prompt, cache_conv_sample)
```

```python
import functools
import math

import jax
import jax.numpy as jnp
from jax import lax
from jax.experimental import pallas as pl
from jax.experimental.pallas import tpu as pltpu

F32 = jnp.float32
BF16 = jnp.bfloat16
U32 = jnp.uint32
I32 = jnp.int32

EPS = 1e-6
GLA_HEADS = 4
GLA_GATE_NORM = 16.0
GLA_CHUNK = 64
CONV_WIDTH = 31
MOE_GROUPS = 8
MOE_PER_GROUP = 8
MOE_TOPK = 2

LANES = 128
VMEM_LIMIT = 56 * 1024 * 1024
EXPERT_ROWS = 512
EXPERT_SUB = 128
EXPERT_FT = 256
COMBINE_ROWS = 128


def _cparams(n_axes, **kw):
    return pltpu.CompilerParams(
        dimension_semantics=("arbitrary",) * n_axes, vmem_limit_bytes=VMEM_LIMIT, **kw)


def _nt(a, b):
    return lax.dot_general(a, b, (((1,), (1,)), ((), ())), preferred_element_type=F32)


def _tn(a, b):
    return lax.dot_general(a, b, (((0,), (0,)), ((), ())), preferred_element_type=F32)


def _dot(a, b):
    return jnp.dot(a, b, preferred_element_type=F32)


def _split_bf16(x):
    hi = x.astype(BF16)
    lo = (x - hi.astype(F32)).astype(BF16)
    return hi, lo


def _sigmoid(x):
    return 1.0 / (1.0 + jnp.exp(-x))


def _rmsnorm_kernel(x_ref, g_ref, o_ref):
    x = x_ref[...]
    y = x * lax.rsqrt(jnp.mean(x * x, axis=-1, keepdims=True) + EPS) * g_ref[...]
    o_ref[...] = y.astype(o_ref.dtype)


def rmsnorm_rows(x, g, layer, tm=512):
    T, D = x.shape
    return pl.pallas_call(
        _rmsnorm_kernel,
        grid=(T // tm,),
        in_specs=[pl.BlockSpec((tm, D), lambda i: (i, 0)),
                  pl.BlockSpec((None, 1, D), lambda i: (layer, 0, 0))],
        out_specs=pl.BlockSpec((tm, D), lambda i: (i, 0)),
        out_shape=jax.ShapeDtypeStruct((T, D), BF16),
        compiler_params=_cparams(1),
        name="rmsnorm",
    )(x, g)


def _mm_kernel(*refs, glu, has_bias, has_res, n1):
    it = iter(refs)
    a_ref = next(it)
    a2_ref = next(it) if n1 is not None else None
    w_ref = next(it)
    w2_ref = next(it) if glu else None
    b_ref = next(it) if has_bias else None
    b2_ref = next(it) if (glu and has_bias) else None
    res_ref = next(it) if has_res else None
    o_ref = next(it)
    wb_ref = next(it)
    wb2_ref = next(it) if glu else None

    @pl.when(pl.program_id(1) == 0)
    def _():
        wb_ref[...] = w_ref[...].astype(BF16)
        if glu:
            wb2_ref[...] = w2_ref[...].astype(BF16)

    def rows_from(src_ref):
        a = src_ref[...]
        acc = _dot(a, wb_ref[...])
        if has_bias:
            acc = acc + b_ref[...]
        if glu:
            gate = _dot(a, wb2_ref[...])
            if has_bias:
                gate = gate + b2_ref[...]
            acc = acc * _sigmoid(gate)
        if has_res:
            acc = acc + res_ref[...]
        o_ref[...] = acc.astype(o_ref.dtype)

    if n1 is None:
        rows_from(a_ref)
    else:
        pl.when(pl.program_id(1) < n1)(lambda: rows_from(a_ref))
        pl.when(pl.program_id(1) >= n1)(lambda: rows_from(a2_ref))


def matmul(a, w, layer, n_out, *, tn, tm=1024, a2=None, bias=None, res=None, glu=False,
           out_dtype=F32, name="matmul"):
    M, K = a.shape
    n1 = None
    if a2 is None:
        in_specs = [pl.BlockSpec((tm, K), lambda j, i: (i, 0))]
        args = [a]
    else:
        assert M % tm == 0 and a2.shape[0] % tm == 0
        n1 = M // tm
        M = M + a2.shape[0]
        in_specs = [pl.BlockSpec((tm, K), lambda j, i: (jnp.minimum(i, n1 - 1), 0)),
                    pl.BlockSpec((tm, K), lambda j, i: (jnp.maximum(i - n1, 0), 0))]
        args = [a, a2]
    nj, ni = n_out // tn, M // tm
    goff = n_out // tn
    in_specs.append(pl.BlockSpec((None, K, tn), lambda j, i: (layer, 0, j)))
    args.append(w)
    if glu:
        in_specs.append(pl.BlockSpec((None, K, tn), lambda j, i: (layer, 0, j + goff)))
        args.append(w)
    if bias is not None:
        in_specs.append(pl.BlockSpec((None, 1, tn), lambda j, i: (layer, 0, j)))
        args.append(bias)
        if glu:
            in_specs.append(pl.BlockSpec((None, 1, tn), lambda j, i: (layer, 0, j + goff)))
            args.append(bias)
    if res is not None:
        in_specs.append(pl.BlockSpec((tm, tn), lambda j, i: (i, j)))
        args.append(res)
    scratch = [pltpu.VMEM((K, tn), BF16)] * (2 if glu else 1)
    return pl.pallas_call(
        functools.partial(_mm_kernel, glu=glu, has_bias=bias is not None, has_res=res is not None,
                          n1=n1),
        grid=(nj, ni),
        in_specs=in_specs,
        out_specs=pl.BlockSpec((tm, tn), lambda j, i: (i, j)),
        out_shape=jax.ShapeDtypeStruct((M, n_out), out_dtype),
        scratch_shapes=scratch,
        compiler_params=_cparams(2),
        name=name,
    )(*args)


def _gla_kernel(*refs, nb, chunk, has_s0, scale):
    it = iter(refs)
    q_ref, k_ref, v_ref, r_ref, gl_ref, wg2_ref, bg2_ref, ng_ref = [next(it) for _ in range(8)]
    s0_ref = next(it) if has_s0 else None
    o_ref, sout_ref, s_ref = next(it), next(it), next(it)

    c = pl.program_id(2)
    rows = nb * chunk
    dk = q_ref.shape[1]
    dv = v_ref.shape[1]
    shift = chunk.bit_length() - 1

    @pl.when(c == 0)
    def _():
        if has_s0:
            s_ref[...] = s0_ref[...]
        else:
            s_ref[...] = jnp.zeros(s_ref.shape, F32)

    q = q_ref[...].astype(F32) * scale
    k = k_ref[...].astype(F32)
    v = v_ref[...]
    z = _dot(gl_ref[...].astype(BF16), wg2_ref[...].astype(BF16)) + bg2_ref[...]
    logf = (jnp.minimum(z, 0.0) - jnp.log1p(jnp.exp(-jnp.abs(z)))) * (1.0 / GLA_GATE_NORM)

    ri = lax.broadcasted_iota(I32, (rows, rows), 0)
    ci = lax.broadcasted_iota(I32, (rows, rows), 1)
    same = (ri >> shift) == (ci >> shift)
    tril = same & (ci <= ri)
    l_hi, l_lo = _split_bf16(logf)
    tril_b = tril.astype(BF16)
    same_b = same.astype(BF16)
    b = _dot(tril_b, l_hi) + _dot(tril_b, l_lo)
    bl = _dot(same_b, l_hi) + _dot(same_b, l_lo)
    qe = (q * jnp.exp(b)).astype(BF16)
    ke = (k * jnp.exp(-b)).astype(BF16)
    kd = k * jnp.exp(bl - b)
    scores = jnp.where(tril, _nt(qe, ke), 0.0)
    o = _dot(scores.astype(BF16), v)

    ones = jnp.ones((rows, LANES), BF16)
    rowseq = lax.broadcasted_iota(I32, (rows, 1), 0) >> shift
    for n in range(nb):
        s_old = s_ref[n]
        o_n = _dot(qe, s_old.astype(BF16))
        if nb > 1:
            mine = rowseq == n
            o = o + jnp.where(mine, o_n, 0.0)
            kd_n = jnp.where(mine, kd, 0.0).astype(BF16)
            lh_n = jnp.where(mine, l_hi, jnp.zeros_like(l_hi))
            ll_n = jnp.where(mine, l_lo, jnp.zeros_like(l_lo))
        else:
            o = o + o_n
            kd_n = kd.astype(BF16)
            lh_n, ll_n = l_hi, l_lo
        decay = jnp.exp(_tn(lh_n, ones) + _tn(ll_n, ones))
        decay = jnp.concatenate([decay] * (dv // LANES), axis=1)
        s_ref[n] = s_old * decay + _tn(kd_n, v)

    o = o * lax.rsqrt(jnp.mean(o * o, axis=-1, keepdims=True) + EPS) * ng_ref[...]
    r = r_ref[...].astype(F32)
    o_ref[...] = (o * (r * _sigmoid(r))).astype(o_ref.dtype)

    @pl.when(c == pl.num_programs(2) - 1)
    def _():
        sout_ref[...] = s_ref[...]


def gla_group(proj, gl, w_g2, b_g2, norm_g, layer, *, row0, batch, seq, nb, dk_total, dv_total,
              s0=None, name="gla"):
    H = GLA_HEADS
    dkh, dvh = dk_total // H, dv_total // H
    chunk = math.gcd(seq, GLA_CHUNK)
    nchunk = seq // chunk
    rows = nb * chunk
    assert chunk & (chunk - 1) == 0 and batch % nb == 0 and row0 % rows == 0
    assert nb == 1 or nchunk == 1
    assert (2 * dk_total) % dvh == 0 and rows % 16 == 0
    rb0 = row0 // rows
    voff = 2 * dk_total // dvh
    roff = voff + H

    def rowblk(bb, c):
        return rb0 + bb * nchunk + c

    in_specs = [
        pl.BlockSpec((rows, dkh), lambda bb, h, c: (rowblk(bb, c), h)),
        pl.BlockSpec((rows, dkh), lambda bb, h, c: (rowblk(bb, c), H + h)),
        pl.BlockSpec((rows, dvh), lambda bb, h, c: (rowblk(bb, c), voff + h)),
        pl.BlockSpec((rows, dvh), lambda bb, h, c: (rowblk(bb, c), roff + h)),
        pl.BlockSpec((rows, LANES), lambda bb, h, c: (rowblk(bb, c), 0)),
        pl.BlockSpec((LANES, dkh), lambda bb, h, c: (0, h)),
        pl.BlockSpec((None, 1, dkh), lambda bb, h, c: (layer, 0, h)),
        pl.BlockSpec((None, 1, dvh), lambda bb, h, c: (layer, 0, 0)),
    ]
    args = [proj, proj, proj, proj, gl, w_g2, b_g2, norm_g]
    if s0 is not None:
        in_specs.append(pl.BlockSpec((None, nb, None, dkh, dvh), lambda bb, h, c: (layer, bb, h, 0, 0)))
        args.append(s0)
    kern = functools.partial(_gla_kernel, nb=nb, chunk=chunk, has_s0=s0 is not None,
                             scale=float(dkh) ** -0.5)
    return pl.pallas_call(
        kern,
        grid=(batch // nb, H, nchunk),
        in_specs=in_specs,
        out_specs=[pl.BlockSpec((rows, dvh), lambda bb, h, c: (bb * nchunk + c, h)),
                   pl.BlockSpec((nb, None, dkh, dvh), lambda bb, h, c: (bb, h, 0, 0))],
        out_shape=[jax.ShapeDtypeStruct((batch * seq, dv_total), BF16),
                   jax.ShapeDtypeStruct((batch, H, dkh, dvh), F32)],
        scratch_shapes=[pltpu.VMEM((nb, dkh, dvh), F32)],
        compiler_params=_cparams(3),
        name=name,
    )(*args)


def _conv_kernel(*refs, nb, tl, has_cache):
    it = iter(refs)
    u_ref, wdw_ref, bdw_ref, lng_ref, lnb_ref = [next(it) for _ in range(5)]
    cache_ref = next(it) if has_cache else None
    z_ref, cout_ref, win_ref, acc_ref = next(it), next(it), next(it), next(it)

    t = pl.program_id(1)
    halo = CONV_WIDTH - 1
    pad = 32
    ch = u_ref.shape[1]

    for n in range(nb):
        @pl.when(t == 0)
        def _():
            win_ref[n, 0:pad, :] = jnp.zeros((pad, ch), F32)
            if has_cache:
                win_ref[n, pad - halo:pad, :] = cache_ref[n]

        @pl.when(t > 0)
        def _():
            win_ref[n, 0:pad, :] = win_ref[n, tl:tl + pad, :]

        win_ref[n, pad:pad + tl, :] = u_ref[n * tl:(n + 1) * tl, :]

        for cb in range(ch // LANES):
            lanes = slice(cb * LANES, (cb + 1) * LANES)
            acc = jnp.zeros((tl, LANES), F32) + bdw_ref[:, lanes]
            for w in range(CONV_WIDTH):
                acc = acc + win_ref[n, pad - halo + w:pad - halo + w + tl, lanes] * wdw_ref[w:w + 1, lanes]
            acc_ref[n * tl:(n + 1) * tl, lanes] = acc

        @pl.when(t == pl.num_programs(1) - 1)
        def _():
            cout_ref[n] = win_ref[n, pad + tl - halo:pad + tl, :]

    zc = acc_ref[...]
    mu = jnp.mean(zc, axis=-1, keepdims=True)
    d = zc - mu
    var = jnp.mean(d * d, axis=-1, keepdims=True)
    y = d * lax.rsqrt(var + EPS) * lng_ref[...] + lnb_ref[...]
    z_ref[...] = (y * _sigmoid(y)).astype(z_ref.dtype)


def conv_group(u, w_dw, b_dw, ln_g, ln_b, layer, *, row0, batch, seq, nb, tl, cache=None,
               name="conv"):
    ch = u.shape[1]
    halo = CONV_WIDTH - 1
    nt = seq // tl
    rows = nb * tl
    assert seq % tl == 0 and batch % nb == 0 and row0 % rows == 0 and rows % 16 == 0
    assert (nb == 1 or nt == 1) and tl % 8 == 0 and tl + 32 - halo >= 0
    rb0 = row0 // rows

    in_specs = [
        pl.BlockSpec((rows, ch), lambda bb, t: (rb0 + bb * nt + t, 0)),
        pl.BlockSpec((None, CONV_WIDTH, ch), lambda bb, t: (layer, 0, 0)),
        pl.BlockSpec((None, 1, ch), lambda bb, t: (layer, 0, 0)),
        pl.BlockSpec((None, 1, ch), lambda bb, t: (layer, 0, 0)),
        pl.BlockSpec((None, 1, ch), lambda bb, t: (layer, 0, 0)),
    ]
    args = [u, w_dw, b_dw, ln_g, ln_b]
    if cache is not None:
        in_specs.append(pl.BlockSpec((None, nb, halo, ch), lambda bb, t: (layer, bb, 0, 0)))
        args.append(cache)
    kern = functools.partial(_conv_kernel, nb=nb, tl=tl, has_cache=cache is not None)
    return pl.pallas_call(
        kern,
        grid=(batch // nb, nt),
        in_specs=in_specs,
        out_specs=[pl.BlockSpec((rows, ch), lambda bb, t: (bb * nt + t, 0)),
                   pl.BlockSpec((nb, halo, ch), lambda bb, t: (bb, 0, 0))],
        out_shape=[jax.ShapeDtypeStruct((batch * seq, ch), BF16),
                   jax.ShapeDtypeStruct((batch, halo, ch), F32)],
        scratch_shapes=[pltpu.VMEM((nb, 32 + tl, ch), F32), pltpu.VMEM((rows, ch), F32)],
        compiler_params=_cparams(2),
        name=name,
    )(*args)


def _router_kernel(x_ref, g_ref, wr_ref, br_ref, hp_ref, ids_ref, gates_ref):
    x = x_ref[...]
    h = x * lax.rsqrt(jnp.mean(x * x, axis=-1, keepdims=True) + EPS) * g_ref[...]
    half = h.shape[1] // 2

    bits = lax.bitcast_convert_type(h.astype(BF16).astype(F32), U32)
    hp_ref[...] = (bits[:, half:] & jnp.uint32(0xFFFF0000)) | (bits[:, :half] >> 16)

    h_hi, h_lo = _split_bf16(h)
    w_hi, w_lo = _split_bf16(wr_ref[...])
    logits = _dot(h_hi, w_hi) + _dot(h_lo, w_hi) + _dot(h_hi, w_lo) + br_ref[...]

    lane = lax.broadcasted_iota(I32, logits.shape, 1).astype(F32)
    neg = jnp.float32(-jnp.inf)
    big = jnp.float32(4 * LANES)
    is_grp = lane < MOE_GROUPS
    gl = jnp.where(is_grp, logits, neg)
    gmax = jnp.max(gl, axis=-1, keepdims=True)
    g_sel = jnp.min(jnp.where(is_grp & (gl == gmax), lane, big), axis=-1, keepdims=True)
    p_g = 1.0 / jnp.sum(jnp.where(is_grp, jnp.exp(gl - gmax), 0.0), axis=-1, keepdims=True)

    lo = MOE_GROUPS + g_sel * MOE_PER_GROUP
    in_grp = (lane >= lo) & (lane < lo + MOE_PER_GROUP)
    el = jnp.where(in_grp, logits, neg)
    v1 = jnp.max(el, axis=-1, keepdims=True)
    j1 = jnp.min(jnp.where(in_grp & (el == v1), lane, big), axis=-1, keepdims=True)
    el2 = jnp.where(lane == j1, neg, el)
    v2 = jnp.max(el2, axis=-1, keepdims=True)
    j2 = jnp.min(jnp.where(in_grp & (lane != j1) & (el2 == v2), lane, big), axis=-1, keepdims=True)
    e2 = jnp.exp(v2 - v1)
    w1 = p_g / (1.0 + e2)
    w2 = p_g * e2 / (1.0 + e2)

    first = lane == 0.0
    ids_ref[...] = jnp.where(first, j1, j2).astype(I32) - MOE_GROUPS
    gates_ref[...] = jnp.where(first, w1, w2)


def router(x, g, w_r, b_r, layer, tm=256):
    T, D = x.shape
    return pl.pallas_call(
        _router_kernel,
        grid=(T // tm,),
        in_specs=[pl.BlockSpec((tm, D), lambda i: (i, 0)),
                  pl.BlockSpec((None, 1, D), lambda i: (layer, 0, 0)),
                  pl.BlockSpec((D, LANES), lambda i: (0, 0)),
                  pl.BlockSpec((1, LANES), lambda i: (0, 0))],
        out_specs=[pl.BlockSpec((tm, D // 2), lambda i: (i, 0)),
                   pl.BlockSpec((tm, LANES), lambda i: (i, 0)),
                   pl.BlockSpec((tm, LANES), lambda i: (i, 0))],
        out_shape=[jax.ShapeDtypeStruct((T, D // 2), U32),
                   jax.ShapeDtypeStruct((T, LANES), I32),
                   jax.ShapeDtypeStruct((T, LANES), F32)],
        compiler_params=_cparams(1),
        name="router",
    )(x, g, w_r, b_r)


def dispatch_tables(ids, n_experts):
    T, K = ids.shape
    A = T * K
    nc_max = A // EXPERT_ROWS + n_experts
    flat_e = ids.reshape(-1)
    order = jnp.argsort(flat_e, stable=True).astype(I32)
    sorted_e = flat_e[order]
    counts = jnp.zeros((n_experts,), I32).at[flat_e].add(1)
    seg_start = jnp.cumsum(counts) - counts
    nch = (counts + EXPERT_ROWS - 1) // EXPERT_ROWS
    ch_end = jnp.cumsum(nch)
    ch_start = ch_end - nch
    n_used = ch_end[-1]
    dest = ch_start[sorted_e] * EXPERT_ROWS + (jnp.arange(A, dtype=I32) - seg_start[sorted_e])
    pos = jnp.zeros((A,), I32).at[order].set(dest).reshape(T, K)
    src = jnp.zeros((nc_max * EXPERT_ROWS,), I32).at[dest].set(order // K)
    cidx = jnp.arange(nc_max, dtype=I32)
    c_eff = jnp.minimum(cidx, n_used - 1)
    chunk_e = jnp.minimum(jnp.searchsorted(ch_end, c_eff, side="right"), n_experts - 1).astype(I32)
    rows = jnp.clip(counts[chunk_e] - (c_eff - ch_start[chunk_e]) * EXPERT_ROWS, 0, EXPERT_ROWS)
    nsub = jnp.where(cidx < n_used, (rows + EXPERT_SUB - 1) // EXPERT_SUB, 0).astype(I32)
    return chunk_e, nsub, n_used.reshape(1).astype(I32), src, pos


def _expert_kernel(ce_ref, nsub_ref, nused_ref, src_ref, hp_hbm, wg_ref, wu_ref, wd_ref, y_ref,
                   xraw_ref, x_ref, h_ref, wgb_ref, wub_ref, wdb_ref, sem):
    c = pl.program_id(0)
    s = pl.program_id(1)
    nused = nused_ref[0]
    nsub = nsub_ref[c]
    nft = h_ref.shape[0]
    half = xraw_ref.shape[1]
    n_subs = EXPERT_ROWS // EXPERT_SUB

    def issue_gather(cc):
        for sub in range(n_subs):
            @pl.when(sub < nsub_ref[cc])
            def _():
                def body(r, carry):
                    row = sub * EXPERT_SUB + r
                    tok = src_ref[cc * EXPERT_ROWS + row]
                    pltpu.make_async_copy(hp_hbm.at[pl.ds(tok, 1), :], xraw_ref.at[pl.ds(row, 1), :],
                                          sem.at[sub]).start()
                    return carry
                lax.fori_loop(0, EXPERT_SUB, body, 0)

    @pl.when((c == 0) & (s == 0))
    def _():
        issue_gather(0)

    @pl.when((s == 0) & (c < nused))
    def _():
        for sub in range(n_subs):
            @pl.when(sub < nsub)
            def _():
                rows = pl.ds(sub * EXPERT_SUB, EXPERT_SUB)
                pltpu.make_async_copy(hp_hbm.at[pl.ds(0, EXPERT_SUB), :], xraw_ref.at[rows, :], sem.at[sub]).wait()
                xu = xraw_ref[rows, :]
                x_ref[rows, 0:half] = lax.bitcast_convert_type(xu << 16, F32).astype(BF16)
                x_ref[rows, half:2 * half] = lax.bitcast_convert_type(
                    xu & jnp.uint32(0xFFFF0000), F32).astype(BF16)

    @pl.when((s == 1) & (c + 1 < nused))
    def _():
        issue_gather(c + 1)

    @pl.when((s < nft) & (c < nused))
    def _():
        wgb_ref[...] = wg_ref[...].astype(BF16)
        wub_ref[...] = wu_ref[...].astype(BF16)
        for sub in range(n_subs):
            @pl.when(sub < nsub)
            def _():
                rows = pl.ds(sub * EXPERT_SUB, EXPERT_SUB)
                x = x_ref[rows, :]
                g = _dot(x, wgb_ref[...])
                u = _dot(x, wub_ref[...])
                h_ref[s, rows, :] = (g * _sigmoid(g) * u).astype(BF16)

    @pl.when((s >= nft) & (c < nused))
    def _():
        wdb_ref[...] = wd_ref[...].astype(BF16)
        for sub in range(n_subs):
            @pl.when(sub < nsub)
            def _():
                rows = pl.ds(sub * EXPERT_SUB, EXPERT_SUB)
                acc = _dot(h_ref[0, rows, :], wdb_ref[0:EXPERT_FT, :])
                for f in range(1, nft):
                    acc = acc + _dot(h_ref[f, rows, :], wdb_ref[f * EXPERT_FT:(f + 1) * EXPERT_FT, :])
                y_ref[rows, :] = acc


def experts(hp, w_gate, w_up, w_down, layer, chunk_e, nsub, n_used, src):
    D, ff = w_gate.shape[2], w_gate.shape[3]
    nc = chunk_e.shape[0]
    nft = ff // EXPERT_FT
    tn = ff
    nnt = D // tn
    assert ff % EXPERT_FT == 0 and D % tn == 0
    last = nft + nnt - 1

    def used(c, nu):
        return c < nu[0]

    def gate_map(c, s, ce, ns, nu, sr):
        return (layer, ce[c], 0, jnp.where(used(c, nu), jnp.minimum(s, nft - 1), nft - 1))

    def down_map(c, s, ce, ns, nu, sr):
        return (layer, ce[c], 0, jnp.where(used(c, nu), jnp.maximum(s - nft, 0), nnt - 1))

    def out_map(c, s, ce, ns, nu, sr):
        return (jnp.minimum(c, nu[0] - 1), jnp.where(used(c, nu), jnp.maximum(s - nft, 0), nnt - 1))

    grid_spec = pltpu.PrefetchScalarGridSpec(
        num_scalar_prefetch=4,
        grid=(nc, last + 1),
        in_specs=[pl.BlockSpec(memory_space=pl.ANY),
                  pl.BlockSpec((None, None, D, EXPERT_FT), gate_map),
                  pl.BlockSpec((None, None, D, EXPERT_FT), gate_map),
                  pl.BlockSpec((None, None, ff, tn), down_map)],
        out_specs=pl.BlockSpec((EXPERT_ROWS, tn), out_map),
        scratch_shapes=[pltpu.VMEM((EXPERT_ROWS, D // 2), U32),
                        pltpu.VMEM((EXPERT_ROWS, D), BF16),
                        pltpu.VMEM((nft, EXPERT_ROWS, EXPERT_FT), BF16),
                        pltpu.VMEM((D, EXPERT_FT), BF16),
                        pltpu.VMEM((D, EXPERT_FT), BF16),
                        pltpu.VMEM((ff, tn), BF16),
                        pltpu.SemaphoreType.DMA((EXPERT_ROWS // EXPERT_SUB,))],
    )
    return pl.pallas_call(
        _expert_kernel,
        grid_spec=grid_spec,
        out_shape=jax.ShapeDtypeStruct((nc * EXPERT_ROWS, D), F32),
        compiler_params=_cparams(2),
        name="experts",
    )(chunk_e, nsub, n_used, src, hp, w_gate, w_up, w_down)


def _combine_kernel(pos_ref, x_ref, gates_ref, g_ref, y_hbm, *refs, want_x):
    it = iter(refs)
    xo_ref = next(it) if want_x else None
    no_ref = next(it)
    ybuf_ref, sem = next(it), next(it)

    i = pl.program_id(0)
    n = pl.num_programs(0)
    slot = i % 2
    nrow = 2 * COMBINE_ROWS

    def issue(blk, sl):
        def body(r, carry):
            p = pos_ref[blk * nrow + r]
            pltpu.make_async_copy(y_hbm.at[pl.ds(p, 1), :], ybuf_ref.at[sl, pl.ds(r, 1), :], sem.at[sl]).start()
            return carry
        lax.fori_loop(0, nrow, body, 0)

    @pl.when(i == 0)
    def _():
        issue(0, 0)

    @pl.when(i + 1 < n)
    def _():
        issue(i + 1, 1 - slot)

    pltpu.make_async_copy(y_hbm.at[pl.ds(0, nrow), :], ybuf_ref.at[slot], sem.at[slot]).wait()

    gates = gates_ref[...]
    y0 = ybuf_ref[slot, 0:COMBINE_ROWS, :]
    y1 = ybuf_ref[slot, COMBINE_ROWS:nrow, :]
    x = x_ref[...] + y0 * gates[:, 0:1] + y1 * gates[:, 1:2]
    if want_x:
        xo_ref[...] = x
    y = x * lax.rsqrt(jnp.mean(x * x, axis=-1, keepdims=True) + EPS) * g_ref[...]
    no_ref[...] = y.astype(no_ref.dtype)


def combine(x, y, pos, gates, g, g_layer, *, want_x, norm_dtype):
    T, D = x.shape
    nblk = T // COMBINE_ROWS
    pos_steps = pos.reshape(nblk, COMBINE_ROWS, MOE_TOPK).transpose(0, 2, 1).reshape(-1)
    out_specs = [pl.BlockSpec((COMBINE_ROWS, D), lambda i, p: (i, 0))]
    out_shape = [jax.ShapeDtypeStruct((T, D), norm_dtype)]
    if want_x:
        out_specs = [pl.BlockSpec((COMBINE_ROWS, D), lambda i, p: (i, 0))] + out_specs
        out_shape = [jax.ShapeDtypeStruct((T, D), F32)] + out_shape
    grid_spec = pltpu.PrefetchScalarGridSpec(
        num_scalar_prefetch=1,
        grid=(nblk,),
        in_specs=[pl.BlockSpec((COMBINE_ROWS, D), lambda i, p: (i, 0)),
                  pl.BlockSpec((COMBINE_ROWS, LANES), lambda i, p: (i, 0)),
                  pl.BlockSpec((None, 1, D), lambda i, p: (g_layer, 0, 0)),
                  pl.BlockSpec(memory_space=pl.ANY)],
        out_specs=out_specs,
        scratch_shapes=[pltpu.VMEM((2, 2 * COMBINE_ROWS, D), F32),
                        pltpu.SemaphoreType.DMA((2,))],
    )
    return pl.pallas_call(
        functools.partial(_combine_kernel, want_x=want_x),
        grid_spec=grid_spec,
        out_shape=out_shape,
        compiler_params=_cparams(1),
        name="combine",
    )(pos_steps, x, gates, g, y)


def moe_layer(x, layer, norm_ffn, w_grp, b_grp, w_exp, b_exp, w_gate, w_up, w_down,
              g_next, g_next_layer, *, want_x, norm_dtype):
    D = x.shape[1]
    n_experts = w_exp.shape[2]
    n_route = MOE_GROUPS + n_experts
    assert n_route <= LANES and n_experts == MOE_GROUPS * MOE_PER_GROUP
    w_r = jnp.pad(jnp.concatenate([w_grp[layer], w_exp[layer]], axis=1), ((0, 0), (0, LANES - n_route)))
    b_r = jnp.pad(jnp.concatenate([b_grp[layer], b_exp[layer]]), (0, LANES - n_route)).reshape(1, LANES)
    hp, ids, gates = router(x, norm_ffn, w_r, b_r, layer)
    chunk_e, nsub, n_used, src, pos = dispatch_tables(ids[:, :MOE_TOPK], n_experts)
    y = experts(hp, w_gate, w_up, w_down, layer, chunk_e, nsub, n_used, src)
    return combine(x, y, pos, gates, g_next, g_next_layer, want_x=want_x, norm_dtype=norm_dtype)


def kernel(x_prompt, x_sample, state_gla, cache_conv, norm_mix, norm_ffn, norm_final,
           gla_w_in, gla_w_g2, gla_b_g2, gla_norm, gla_w_o,
           conv_w_pw1, conv_b_pw1, conv_w_dw, conv_b_dw, conv_ln_g, conv_ln_b, conv_w_pw2, conv_b_pw2,
           moe_w_grp, moe_b_grp, moe_w_exp, moe_b_exp, moe_w_gate, moe_w_up, moe_w_down):
    Bp, Lp, D = x_prompt.shape
    Bs, Ls, _ = x_sample.shape
    Tp, Ts = Bp * Lp, Bs * Ls
    depth = norm_mix.shape[0]
    assert depth == 2, "layer 0 is the GLA mixer, layer 1 the convolution mixer"
    dk = gla_w_g2.shape[2]
    dv = gla_w_o.shape[1]
    rank = gla_w_g2.shape[1]
    ch = conv_w_dw.shape[2]
    halo = CONV_WIDTH - 1

    def vec3(a):
        return a.reshape(a.shape[0], 1, a.shape[1])

    norm_mix3, norm_ffn3, norm_final3 = vec3(norm_mix), vec3(norm_ffn), norm_final.reshape(1, 1, D)
    x = jnp.concatenate([x_prompt.reshape(Tp, D), x_sample.reshape(Ts, D)], axis=0)
    tm = math.gcd(math.gcd(Tp, Ts), 1024)
    sample_nb = 16 // math.gcd(Ls, 16)

    h = rmsnorm_rows(x, norm_mix3, 0)
    proj = matmul(h, gla_w_in, 0, 2 * dk + 2 * dv, tn=512, tm=tm, out_dtype=BF16, name="gla_proj")
    w_gl = jnp.pad(gla_w_in[0, :, 2 * dk + 2 * dv:], ((0, 0), (0, LANES - rank)))[None]
    gl = matmul(h, w_gl, 0, LANES, tn=LANES, tm=tm, name="gla_gate_proj")
    w_g2 = jnp.pad(gla_w_g2[0], ((0, LANES - rank), (0, 0)))
    gla_args = (proj, gl, w_g2, vec3(gla_b_g2), vec3(gla_norm), 0)
    o_p, s_prompt = gla_group(*gla_args, row0=0, batch=Bp, seq=Lp, nb=1, dk_total=dk, dv_total=dv,
                              name="gla_prompt")
    o_s, s_sample = gla_group(*gla_args, row0=Tp, batch=Bs, seq=Ls, nb=sample_nb, dk_total=dk,
                              dv_total=dv, s0=state_gla, name="gla_sample")
    x = matmul(o_p, gla_w_o, 0, D, tn=512, tm=tm // 2, a2=o_s, res=x, name="gla_out")
    x, h = moe_layer(x, 0, norm_ffn3, moe_w_grp, moe_b_grp, moe_w_exp, moe_b_exp,
                     moe_w_gate, moe_w_up, moe_w_down, norm_mix3, 1, want_x=True, norm_dtype=BF16)

    u = matmul(h, conv_w_pw1, 0, ch, tn=256, tm=tm, bias=vec3(conv_b_pw1), glu=True, name="conv_pw1")
    conv_args = (u, conv_w_dw, vec3(conv_b_dw), vec3(conv_ln_g), vec3(conv_ln_b), 0)
    z_p, c_prompt = conv_group(*conv_args, row0=0, batch=Bp, seq=Lp, nb=1, tl=math.gcd(Lp, 64),
                               name="conv_prompt")
    z_s, c_sample = conv_group(*conv_args, row0=Tp, batch=Bs, seq=Ls, nb=sample_nb, tl=Ls,
                               cache=cache_conv, name="conv_sample")
    x = matmul(z_p, conv_w_pw2, 0, D, tn=512, tm=tm // 2, a2=z_s, bias=vec3(conv_b_pw2), res=x,
               name="conv_pw2")
    (y,) = moe_layer(x, 1, norm_ffn3, moe_w_grp, moe_b_grp, moe_w_exp, moe_b_exp,
                     moe_w_gate, moe_w_up, moe_w_down, norm_final3, 0, want_x=False, norm_dtype=F32)

    y_prompt = y[:Tp].reshape(Bp, Lp, D)
    y_sample = y[Tp:].reshape(Bs, Ls, D)
    return (y_prompt, y_sample, s_prompt[None], s_sample[None], c_prompt[None], c_sample[None])
```

```python
import functools
import math

import jax
import jax.numpy as jnp
from jax import lax
from jax.experimental import pallas as pl
from jax.experimental.pallas import tpu as pltpu

F32 = jnp.float32
BF16 = jnp.bfloat16
U32 = jnp.uint32
I32 = jnp.int32

EPS = 1e-6
GLA_HEADS = 4
GLA_GATE_NORM = 16.0
GLA_CHUNK = 64
CONV_WIDTH = 31
MOE_GROUPS = 8
MOE_PER_GROUP = 8
MOE_TOPK = 2

LANES = 128
SUBLANES = 8
VMEM_LIMIT = 56 * 1024 * 1024
EXPERT_ROWS = 512
EXPERT_SUB = 128
EXPERT_FT = 256
COMBINE_ROWS = 256
DMA_ISSUE_UNROLL = 8


def _cparams(n_axes, **kw):
    return pltpu.CompilerParams(
        dimension_semantics=("arbitrary",) * n_axes, vmem_limit_bytes=VMEM_LIMIT, **kw)


def _nt(a, b):
    return lax.dot_general(a, b, (((1,), (1,)), ((), ())), preferred_element_type=F32)


def _tn(a, b):
    return lax.dot_general(a, b, (((0,), (0,)), ((), ())), preferred_element_type=F32)


def _dot(a, b):
    return jnp.dot(a, b, preferred_element_type=F32)


def _split_bf16(x):
    hi = x.astype(BF16)
    lo = (x - hi.astype(F32)).astype(BF16)
    return hi, lo


def _sigmoid(x):
    return 1.0 / (1.0 + jnp.exp(-x))


def _rms(x, g):
    return x * lax.rsqrt(jnp.mean(x * x, axis=-1, keepdims=True) + EPS) * g


def _two_group_specs(block, n1):
    return (pl.BlockSpec(block, lambda *ix: (jnp.minimum(ix[-1], n1 - 1), 0)),
            pl.BlockSpec(block, lambda *ix: (jnp.maximum(ix[-1] - n1, 0), 0)))


def _rmsnorm_kernel(xp_ref, xs_ref, g_ref, o_ref, *, n1):
    def rows_from(x_ref):
        o_ref[...] = _rms(x_ref[...], g_ref[...]).astype(o_ref.dtype)

    pl.when(pl.program_id(0) < n1)(lambda: rows_from(xp_ref))
    pl.when(pl.program_id(0) >= n1)(lambda: rows_from(xs_ref))


def rmsnorm_rows(x_p, x_s, g, layer, tm):
    D = x_p.shape[1]
    n1 = x_p.shape[0] // tm
    T = x_p.shape[0] + x_s.shape[0]
    return pl.pallas_call(
        functools.partial(_rmsnorm_kernel, n1=n1),
        grid=(T // tm,),
        in_specs=[*_two_group_specs((tm, D), n1),
                  pl.BlockSpec((None, 1, D), lambda i: (layer, 0, 0))],
        out_specs=pl.BlockSpec((tm, D), lambda i: (i, 0)),
        out_shape=jax.ShapeDtypeStruct((T, D), BF16),
        compiler_params=_cparams(1),
        name="rmsnorm",
    )(x_p, x_s, g)


def _mm_kernel(*refs, glu, has_bias, n_res, n1, w_t, n_valid):
    it = iter(refs)
    a_ref = next(it)
    a2_ref = next(it) if n1 is not None else None
    w_ref = next(it)
    w2_ref = next(it) if glu else None
    b_ref = next(it) if has_bias else None
    b2_ref = next(it) if (glu and has_bias) else None
    res_ref = next(it) if n_res >= 1 else None
    res2_ref = next(it) if n_res == 2 else None
    o_ref = next(it)
    wb_ref = next(it)
    wb2_ref = next(it) if glu else None

    @pl.when(pl.program_id(1) == 0)
    def _():
        wb_ref[...] = w_ref[...].astype(BF16)
        if glu:
            wb2_ref[...] = w2_ref[...].astype(BF16)

    mm = _nt if w_t else _dot

    def rows_from(src_ref, r_ref):
        a = src_ref[...]
        acc = mm(a, wb_ref[...])
        if n_valid is not None:
            col = lax.broadcasted_iota(I32, acc.shape, 1)
            acc = jnp.where(col < n_valid, acc, 0.0)
        if has_bias:
            acc = acc + b_ref[...]
        if glu:
            gate = mm(a, wb2_ref[...])
            if has_bias:
                gate = gate + b2_ref[...]
            acc = acc * _sigmoid(gate)
        if r_ref is not None:
            acc = acc + r_ref[...]
        o_ref[...] = acc.astype(o_ref.dtype)

    if n1 is None:
        rows_from(a_ref, res_ref)
    else:
        second = res2_ref if n_res == 2 else res_ref
        pl.when(pl.program_id(1) < n1)(lambda: rows_from(a_ref, res_ref))
        pl.when(pl.program_id(1) >= n1)(lambda: rows_from(a2_ref, second))


def matmul(a, w, layer, n_out, *, tn, tm, a2=None, bias=None, res=None, res2=None, glu=False,
           w_t=False, col0=0, n_valid=None, out_dtype=F32, name="matmul"):
    M, K = a.shape
    n1 = None
    if a2 is None:
        in_specs = [pl.BlockSpec((tm, K), lambda j, i: (i, 0))]
        args = [a]
    else:
        assert M % tm == 0 and a2.shape[0] % tm == 0
        n1 = M // tm
        M = M + a2.shape[0]
        in_specs = list(_two_group_specs((tm, K), n1))
        args = [a, a2]
    assert col0 % tn == 0 and M % tm == 0
    nj, ni = n_out // tn, M // tm
    j0 = col0 // tn
    goff = j0 + n_out // tn
    if w_t:
        wblock = (None, tn, K)
        wmap = lambda off: (lambda j, i: (layer, j + off, 0))
    else:
        wblock = (None, K, tn)
        wmap = lambda off: (lambda j, i: (layer, 0, j + off))
    in_specs.append(pl.BlockSpec(wblock, wmap(j0)))
    args.append(w)
    if glu:
        in_specs.append(pl.BlockSpec(wblock, wmap(goff)))
        args.append(w)
    if bias is not None:
        in_specs.append(pl.BlockSpec((None, 1, tn), lambda j, i: (layer, 0, j + j0)))
        args.append(bias)
        if glu:
            in_specs.append(pl.BlockSpec((None, 1, tn), lambda j, i: (layer, 0, j + goff)))
            args.append(bias)
    n_res = 0
    if res is not None and res2 is None:
        n_res = 1
        in_specs.append(pl.BlockSpec((tm, tn), lambda j, i: (i, j)))
        args.append(res)
    elif res is not None:
        n_res = 2
        in_specs.append(pl.BlockSpec((tm, tn), lambda j, i: (jnp.minimum(i, n1 - 1), j)))
        in_specs.append(pl.BlockSpec((tm, tn), lambda j, i: (jnp.maximum(i - n1, 0), j)))
        args += [res, res2]
    scratch = [pltpu.VMEM(wblock[1:], BF16)] * (2 if glu else 1)
    return pl.pallas_call(
        functools.partial(_mm_kernel, glu=glu, has_bias=bias is not None, n_res=n_res, n1=n1,
                          w_t=w_t, n_valid=n_valid),
        grid=(nj, ni),
        in_specs=in_specs,
        out_specs=pl.BlockSpec((tm, tn), lambda j, i: (i, j)),
        out_shape=jax.ShapeDtypeStruct((M, n_out), out_dtype),
        scratch_shapes=scratch,
        compiler_params=_cparams(2),
        name=name,
    )(*args)


def _gla_kernel(*refs, nb, chunk, has_s0, scale):
    it = iter(refs)
    q_ref, k_ref, v_ref, r_ref, gl_ref, wg2_ref, bg2_ref, ng_ref = [next(it) for _ in range(8)]
    s0_ref = next(it) if has_s0 else None
    o_ref, sout_ref, s_ref = next(it), next(it), next(it)

    c = pl.program_id(2)
    rows = nb * chunk
    dv = v_ref.shape[1]
    shift = chunk.bit_length() - 1

    @pl.when(c == 0)
    def _():
        if has_s0:
            s_ref[...] = s0_ref[...]
        else:
            s_ref[...] = jnp.zeros(s_ref.shape, F32)

    q = q_ref[...].astype(F32) * scale
    k = k_ref[...].astype(F32)
    v = v_ref[...]
    z = _dot(gl_ref[...].astype(BF16), wg2_ref[...].astype(BF16)) + bg2_ref[...]
    logf = (jnp.minimum(z, 0.0) - jnp.log1p(jnp.exp(-jnp.abs(z)))) * (1.0 / GLA_GATE_NORM)

    ri = lax.broadcasted_iota(I32, (rows, rows), 0)
    ci = lax.broadcasted_iota(I32, (rows, rows), 1)
    same = (ri >> shift) == (ci >> shift)
    tril = same & (ci <= ri)
    l_hi, l_lo = _split_bf16(logf)
    tril_b = tril.astype(BF16)
    same_b = same.astype(BF16)
    b = _dot(tril_b, l_hi) + _dot(tril_b, l_lo)
    bl = _dot(same_b, l_hi) + _dot(same_b, l_lo)
    qe = (q * jnp.exp(b)).astype(BF16)
    ke = (k * jnp.exp(-b)).astype(BF16)
    kd = k * jnp.exp(bl - b)
    scores = jnp.where(tril, _nt(qe, ke), 0.0)
    o = _dot(scores.astype(BF16), v)

    ones = jnp.ones((rows, LANES), BF16)
    rowseq = lax.broadcasted_iota(I32, (rows, 1), 0) >> shift
    for n in range(nb):
        s_old = s_ref[n]
        o_n = _dot(qe, s_old.astype(BF16))
        if nb > 1:
            mine = rowseq == n
            o = o + jnp.where(mine, o_n, 0.0)
            kd_n = jnp.where(mine, kd, 0.0).astype(BF16)
            lh_n = jnp.where(mine, l_hi, jnp.zeros_like(l_hi))
            ll_n = jnp.where(mine, l_lo, jnp.zeros_like(l_lo))
        else:
            o = o + o_n
            kd_n = kd.astype(BF16)
            lh_n, ll_n = l_hi, l_lo
        decay = jnp.exp(_tn(lh_n, ones) + _tn(ll_n, ones))
        decay = jnp.concatenate([decay] * (dv // LANES), axis=1)
        s_ref[n] = s_old * decay + _tn(kd_n, v)

    o = _rms(o, ng_ref[...])
    r = r_ref[...].astype(F32)
    o_ref[...] = (o * (r * _sigmoid(r))).astype(o_ref.dtype)

    @pl.when(c == pl.num_programs(2) - 1)
    def _():
        sout_ref[...] = s_ref[...]


def gla_group(proj, gl, w_g2, b_g2, norm_g, layer, *, row0, batch, seq, nb, dk_total, dv_total,
              s0=None, name="gla"):
    H = GLA_HEADS
    dkh, dvh = dk_total // H, dv_total // H
    chunk = math.gcd(seq, GLA_CHUNK)
    nchunk = seq // chunk
    rows = nb * chunk
    assert chunk & (chunk - 1) == 0 and batch % nb == 0 and row0 % rows == 0
    assert nb == 1 or nchunk == 1
    assert (2 * dk_total) % dvh == 0 and rows % 16 == 0
    rb0 = row0 // rows
    voff = 2 * dk_total // dvh
    roff = voff + H

    def rowblk(bb, c):
        return rb0 + bb * nchunk + c

    in_specs = [
        pl.BlockSpec((rows, dkh), lambda bb, h, c: (rowblk(bb, c), h)),
        pl.BlockSpec((rows, dkh), lambda bb, h, c: (rowblk(bb, c), H + h)),
        pl.BlockSpec((rows, dvh), lambda bb, h, c: (rowblk(bb, c), voff + h)),
        pl.BlockSpec((rows, dvh), lambda bb, h, c: (rowblk(bb, c), roff + h)),
        pl.BlockSpec((rows, LANES), lambda bb, h, c: (rowblk(bb, c), 0)),
        pl.BlockSpec((LANES, dkh), lambda bb, h, c: (0, h)),
        pl.BlockSpec((None, 1, dkh), lambda bb, h, c: (layer, 0, h)),
        pl.BlockSpec((None, 1, dvh), lambda bb, h, c: (layer, 0, 0)),
    ]
    args = [proj, proj, proj, proj, gl, w_g2, b_g2, norm_g]
    if s0 is not None:
        in_specs.append(pl.BlockSpec((None, nb, None, dkh, dvh), lambda bb, h, c: (layer, bb, h, 0, 0)))
        args.append(s0)
    kern = functools.partial(_gla_kernel, nb=nb, chunk=chunk, has_s0=s0 is not None,
                             scale=float(dkh) ** -0.5)
    return pl.pallas_call(
        kern,
        grid=(batch // nb, H, nchunk),
        in_specs=in_specs,
        out_specs=[pl.BlockSpec((rows, dvh), lambda bb, h, c: (bb * nchunk + c, h)),
                   pl.BlockSpec((nb, None, dkh, dvh), lambda bb, h, c: (bb, h, 0, 0))],
        out_shape=[jax.ShapeDtypeStruct((batch * seq, dv_total), BF16),
                   jax.ShapeDtypeStruct((batch, H, dkh, dvh), F32)],
        scratch_shapes=[pltpu.VMEM((nb, dkh, dvh), F32)],
        compiler_params=_cparams(3),
        name=name,
    )(*args)


CONV_HALO = CONV_WIDTH - 1
CONV_PAD = -(-CONV_HALO // SUBLANES) * SUBLANES
CONV_SHIFT_ROWS = (CONV_PAD // SUBLANES - 1) * SUBLANES


def _conv_kernel(*refs, nb, tl, has_cache):
    it = iter(refs)
    u_ref, wdw_ref, bdw_ref, lng_ref, lnb_ref = [next(it) for _ in range(5)]
    cache_ref = next(it) if has_cache else None
    z_ref, cout_ref, win_ref, sh_ref, acc_ref = [next(it) for _ in range(5)]

    t = pl.program_id(1)
    ch = u_ref.shape[1]
    first = CONV_PAD - CONV_HALO

    for n in range(nb):
        @pl.when(t == 0)
        def _():
            win_ref[n, 0:CONV_PAD, :] = jnp.zeros((CONV_PAD, ch), F32)
            if has_cache:
                win_ref[n, first:CONV_PAD, :] = cache_ref[n]

        @pl.when(t > 0)
        def _():
            win_ref[n, 0:CONV_PAD, :] = win_ref[n, tl:tl + CONV_PAD, :]

        win_ref[n, CONV_PAD:CONV_PAD + tl, :] = u_ref[n * tl:(n + 1) * tl, :]

        for p in range(1, SUBLANES):
            sh_ref[p - 1] = win_ref[n, p:p + tl + CONV_SHIFT_ROWS, :]

        for cb in range(ch // LANES):
            lanes = slice(cb * LANES, (cb + 1) * LANES)
            acc = jnp.zeros((tl, LANES), F32) + bdw_ref[:, lanes]
            for w in range(CONV_WIDTH):
                off = first + w
                a, p = off // SUBLANES * SUBLANES, off % SUBLANES
                rows = win_ref[n, a:a + tl, lanes] if p == 0 else sh_ref[p - 1, a:a + tl, lanes]
                acc = acc + rows * wdw_ref[w:w + 1, lanes]
            acc_ref[n * tl:(n + 1) * tl, lanes] = acc

        @pl.when(t == pl.num_programs(1) - 1)
        def _():
            cout_ref[n] = win_ref[n, CONV_PAD + tl - CONV_HALO:CONV_PAD + tl, :]

    zc = acc_ref[...]
    mu = jnp.mean(zc, axis=-1, keepdims=True)
    d = zc - mu
    var = jnp.mean(d * d, axis=-1, keepdims=True)
    y = d * lax.rsqrt(var + EPS) * lng_ref[...] + lnb_ref[...]
    z_ref[...] = (y * _sigmoid(y)).astype(z_ref.dtype)


def conv_group(u, w_dw, b_dw, ln_g, ln_b, layer, *, row0, batch, seq, nb, tl, cache=None,
               name="conv"):
    ch = u.shape[1]
    nt = seq // tl
    rows = nb * tl
    assert seq % tl == 0 and batch % nb == 0 and row0 % rows == 0 and rows % 16 == 0
    assert (nb == 1 or nt == 1) and tl % SUBLANES == 0
    rb0 = row0 // rows

    in_specs = [
        pl.BlockSpec((rows, ch), lambda bb, t: (rb0 + bb * nt + t, 0)),
        pl.BlockSpec((None, CONV_WIDTH, ch), lambda bb, t: (layer, 0, 0)),
        pl.BlockSpec((None, 1, ch), lambda bb, t: (layer, 0, 0)),
        pl.BlockSpec((None, 1, ch), lambda bb, t: (layer, 0, 0)),
        pl.BlockSpec((None, 1, ch), lambda bb, t: (layer, 0, 0)),
    ]
    args = [u, w_dw, b_dw, ln_g, ln_b]
    if cache is not None:
        in_specs.append(pl.BlockSpec((None, nb, CONV_HALO, ch), lambda bb, t: (layer, bb, 0, 0)))
        args.append(cache)
    kern = functools.partial(_conv_kernel, nb=nb, tl=tl, has_cache=cache is not None)
    return pl.pallas_call(
        kern,
        grid=(batch // nb, nt),
        in_specs=in_specs,
        out_specs=[pl.BlockSpec((rows, ch), lambda bb, t: (bb * nt + t, 0)),
                   pl.BlockSpec((nb, CONV_HALO, ch), lambda bb, t: (bb, 0, 0))],
        out_shape=[jax.ShapeDtypeStruct((batch * seq, ch), BF16),
                   jax.ShapeDtypeStruct((batch, CONV_HALO, ch), F32)],
        scratch_shapes=[pltpu.VMEM((nb, CONV_PAD + tl, ch), F32),
                        pltpu.VMEM((SUBLANES - 1, tl + CONV_SHIFT_ROWS, ch), F32),
                        pltpu.VMEM((rows, ch), F32)],
        compiler_params=_cparams(2),
        name=name,
    )(*args)


def _router_kernel(x_ref, g_ref, wr_ref, br_ref, hp_ref, ids_ref, gates_ref):
    h = _rms(x_ref[...], g_ref[...])
    half = h.shape[1] // 2

    bits = lax.bitcast_convert_type(h.astype(BF16).astype(F32), U32)
    hp_ref[...] = (bits[:, half:] & jnp.uint32(0xFFFF0000)) | (bits[:, :half] >> 16)

    h_hi, h_lo = _split_bf16(h)
    w_hi, w_lo = _split_bf16(wr_ref[...])
    logits = _dot(h_hi, w_hi) + _dot(h_lo, w_hi) + _dot(h_hi, w_lo) + br_ref[...]

    lane = lax.broadcasted_iota(I32, logits.shape, 1).astype(F32)
    neg = jnp.float32(-jnp.inf)
    big = jnp.float32(4 * LANES)
    is_grp = lane < MOE_GROUPS
    gl = jnp.where(is_grp, logits, neg)
    gmax = jnp.max(gl, axis=-1, keepdims=True)
    g_sel = jnp.min(jnp.where(is_grp & (gl == gmax), lane, big), axis=-1, keepdims=True)
    p_g = 1.0 / jnp.sum(jnp.where(is_grp, jnp.exp(gl - gmax), 0.0), axis=-1, keepdims=True)

    lo = MOE_GROUPS + g_sel * MOE_PER_GROUP
    in_grp = (lane >= lo) & (lane < lo + MOE_PER_GROUP)
    el = jnp.where(in_grp, logits, neg)
    v1 = jnp.max(el, axis=-1, keepdims=True)
    j1 = jnp.min(jnp.where(in_grp & (el == v1), lane, big), axis=-1, keepdims=True)
    el2 = jnp.where(lane == j1, neg, el)
    v2 = jnp.max(el2, axis=-1, keepdims=True)
    j2 = jnp.min(jnp.where(in_grp & (lane != j1) & (el2 == v2), lane, big), axis=-1, keepdims=True)
    e2 = jnp.exp(v2 - v1)
    w1 = p_g / (1.0 + e2)
    w2 = p_g * e2 / (1.0 + e2)

    first = lane == 0.0
    ids_ref[...] = jnp.where(first, j1, j2).astype(I32) - MOE_GROUPS
    gates_ref[...] = jnp.where(first, w1, w2)


def router(x, g, w_r, b_r, layer, tm=256):
    T, D = x.shape
    return pl.pallas_call(
        _router_kernel,
        grid=(T // tm,),
        in_specs=[pl.BlockSpec((tm, D), lambda i: (i, 0)),
                  pl.BlockSpec((None, 1, D), lambda i: (layer, 0, 0)),
                  pl.BlockSpec((D, LANES), lambda i: (0, 0)),
                  pl.BlockSpec((1, LANES), lambda i: (0, 0))],
        out_specs=[pl.BlockSpec((tm, D // 2), lambda i: (i, 0)),
                   pl.BlockSpec((tm, LANES), lambda i: (i, 0)),
                   pl.BlockSpec((tm, LANES), lambda i: (i, 0))],
        out_shape=[jax.ShapeDtypeStruct((T, D // 2), U32),
                   jax.ShapeDtypeStruct((T, LANES), I32),
                   jax.ShapeDtypeStruct((T, LANES), F32)],
        compiler_params=_cparams(1),
        name="router",
    )(x, g, w_r, b_r)


def dispatch_tables(ids, n_experts):
    T, K = ids.shape
    A = T * K
    nc_max = A // EXPERT_ROWS + n_experts
    flat_e = ids.reshape(-1)
    order = jnp.argsort(flat_e).astype(I32)
    counts = jnp.sum((flat_e[:, None] == jnp.arange(n_experts, dtype=I32)[None, :]).astype(I32), axis=0)
    seg_start = jnp.cumsum(counts) - counts
    nch = (counts + EXPERT_ROWS - 1) // EXPERT_ROWS
    ch_end = jnp.cumsum(nch)
    ch_start = ch_end - nch
    n_used = ch_end[-1]
    cidx = jnp.arange(nc_max, dtype=I32)
    c_eff = jnp.minimum(cidx, n_used - 1)
    chunk_e = jnp.minimum(jnp.sum((ch_end[None, :] <= c_eff[:, None]).astype(I32), axis=1), n_experts - 1)
    sel = (chunk_e[:, None] == jnp.arange(n_experts, dtype=I32)[None, :]).astype(I32)
    k_in_e = c_eff - jnp.sum(sel * ch_start[None, :], axis=1)
    cnt_e = jnp.sum(sel * counts[None, :], axis=1)
    nrows = jnp.where(cidx < n_used, jnp.clip(cnt_e - k_in_e * EXPERT_ROWS, 0, EXPERT_ROWS), 0)
    base = jnp.sum(sel * seg_start[None, :], axis=1) + k_in_e * EXPERT_ROWS
    return (chunk_e.astype(I32), nrows.astype(I32), base.astype(I32),
            n_used.reshape(1).astype(I32), order)


def _expert_kernel(ce_ref, nrows_ref, base_ref, nused_ref, order_ref, hp_hbm, wg_ref, wu_ref,
                   wd_ref, yt_hbm, xraw_ref, x_ref, y_ref, gsem, ssem, *, n_tok, n_assign):
    c = pl.program_id(0)
    s = pl.program_id(1)
    nused = nused_ref[0]
    nrows = nrows_ref[c]
    nsub = (nrows + EXPERT_SUB - 1) // EXPERT_SUB
    half = xraw_ref.shape[1]
    n_subs = EXPERT_ROWS // EXPERT_SUB
    last_s = pl.num_programs(1) - 1
    slot = c % 2

    def assignment(cc, row):
        return order_ref[base_ref[cc] + jnp.minimum(row, nrows_ref[cc] - 1)]

    def issue_gather(cc):
        for sub in range(n_subs):
            @pl.when(sub * EXPERT_SUB < nrows_ref[cc])
            def _():
                def body(r, carry):
                    row = sub * EXPERT_SUB + r
                    tok = assignment(cc, row) >> 1
                    pltpu.make_async_copy(hp_hbm.at[pl.ds(tok, 1), :], xraw_ref.at[pl.ds(row, 1), :],
                                          gsem.at[sub]).start()
                    return carry
                lax.fori_loop(0, EXPERT_SUB, body, 0, unroll=DMA_ISSUE_UNROLL)

    def issue_scatter(cc, sl):
        for sub in range(n_subs):
            @pl.when(sub * EXPERT_SUB < nrows_ref[cc])
            def _():
                def body(r, carry):
                    row = sub * EXPERT_SUB + r
                    a = assignment(cc, row)
                    dst = jnp.where(row < nrows_ref[cc], (a & 1) * n_tok + (a >> 1),
                                    n_assign + cc * EXPERT_SUB + r)
                    pltpu.make_async_copy(y_ref.at[sl, pl.ds(row, 1), :], yt_hbm.at[pl.ds(dst, 1), :],
                                          ssem.at[sl, sub]).start()
                    return carry
                lax.fori_loop(0, EXPERT_SUB, body, 0, unroll=DMA_ISSUE_UNROLL)

    def wait_scatter(cc, sl):
        for sub in range(n_subs):
            @pl.when(sub * EXPERT_SUB < nrows_ref[cc])
            def _():
                rows = pl.ds(sub * EXPERT_SUB, EXPERT_SUB)
                pltpu.make_async_copy(y_ref.at[sl, rows, :], yt_hbm.at[rows, :], ssem.at[sl, sub]).wait()

    @pl.when((c == 0) & (s == 0))
    def _():
        issue_gather(0)

    @pl.when((s == 0) & (c < nused))
    def _():
        for sub in range(n_subs):
            @pl.when(sub < nsub)
            def _():
                rows = pl.ds(sub * EXPERT_SUB, EXPERT_SUB)
                pltpu.make_async_copy(hp_hbm.at[rows, :], xraw_ref.at[rows, :], gsem.at[sub]).wait()
                xu = xraw_ref[rows, :]
                x_ref[rows, 0:half] = lax.bitcast_convert_type(xu << 16, F32).astype(BF16)
                x_ref[rows, half:2 * half] = lax.bitcast_convert_type(
                    xu & jnp.uint32(0xFFFF0000), F32).astype(BF16)

        @pl.when(c >= 2)
        def _():
            wait_scatter(c - 2, slot)

    @pl.when((s == jnp.minimum(1, last_s)) & (c + 1 < nused))
    def _():
        issue_gather(c + 1)

    @pl.when(c < nused)
    def _():
        for m in range(1, n_subs + 1):
            @pl.when(nsub == m)
            def _():
                rows = slice(0, m * EXPERT_SUB)
                x = x_ref[rows, :]
                g = _dot(x, wg_ref[...].astype(BF16))
                u = _dot(x, wu_ref[...].astype(BF16))
                hcol = (g * _sigmoid(g) * u).astype(BF16)
                part = _dot(hcol, wd_ref[...].astype(BF16))

                @pl.when(s == 0)
                def _():
                    y_ref[slot, rows, :] = part

                @pl.when(s > 0)
                def _():
                    y_ref[slot, rows, :] = y_ref[slot, rows, :] + part

        @pl.when(s == last_s)
        def _():
            issue_scatter(c, slot)

    @pl.when((c == pl.num_programs(0) - 1) & (s == last_s))
    def _():
        wait_scatter(nused - 1, (nused - 1) % 2)

        @pl.when(nused >= 2)
        def _():
            wait_scatter(nused - 2, nused % 2)


def experts(hp, w_gate, w_up, w_down, layer, chunk_e, nrows, base, n_used, order):
    T = hp.shape[0]
    D, ff = w_gate.shape[2], w_gate.shape[3]
    nc = chunk_e.shape[0]
    A = order.shape[0]
    nft = ff // EXPERT_FT
    assert ff % EXPERT_FT == 0 and A == T * MOE_TOPK and MOE_TOPK == 2

    def f_eff(c, s, nu):
        return jnp.where(c < nu[0], s, nft - 1)

    def gate_map(c, s, ce, nr, ba, nu, od):
        return (layer, ce[c], 0, f_eff(c, s, nu))

    def down_map(c, s, ce, nr, ba, nu, od):
        return (layer, ce[c], f_eff(c, s, nu), 0)

    grid_spec = pltpu.PrefetchScalarGridSpec(
        num_scalar_prefetch=5,
        grid=(nc, nft),
        in_specs=[pl.BlockSpec(memory_space=pl.ANY),
                  pl.BlockSpec((None, None, D, EXPERT_FT), gate_map),
                  pl.BlockSpec((None, None, D, EXPERT_FT), gate_map),
                  pl.BlockSpec((None, None, EXPERT_FT, D), down_map)],
        out_specs=pl.BlockSpec(memory_space=pl.ANY),
        scratch_shapes=[pltpu.VMEM((EXPERT_ROWS, D // 2), U32),
                        pltpu.VMEM((EXPERT_ROWS, D), BF16),
                        pltpu.VMEM((2, EXPERT_ROWS, D), F32),
                        pltpu.SemaphoreType.DMA((EXPERT_ROWS // EXPERT_SUB,)),
                        pltpu.SemaphoreType.DMA((2, EXPERT_ROWS // EXPERT_SUB))],
    )
    return pl.pallas_call(
        functools.partial(_expert_kernel, n_tok=T, n_assign=A),
        grid_spec=grid_spec,
        out_shape=jax.ShapeDtypeStruct((A + nc * EXPERT_SUB, D), F32),
        compiler_params=_cparams(2),
        name="experts",
    )(chunk_e, nrows, base, n_used, order, hp, w_gate, w_up, w_down)


def _combine_kernel(x_ref, y0_ref, y1_ref, gates_ref, g_ref, *out_refs, want_x, n1):
    gates = gates_ref[...]
    x = x_ref[...] + y0_ref[...] * gates[:, 0:1] + y1_ref[...] * gates[:, 1:2]
    it = iter(out_refs)
    if want_x:
        next(it)[...] = x
    y = _rms(x, g_ref[...])
    if n1 is None:
        o_ref = next(it)
        o_ref[...] = y.astype(o_ref.dtype)
    else:
        op_ref, os_ref = next(it), next(it)

        @pl.when(pl.program_id(0) < n1)
        def _():
            op_ref[...] = y.astype(op_ref.dtype)

        @pl.when(pl.program_id(0) >= n1)
        def _():
            os_ref[...] = y.astype(os_ref.dtype)


def combine(x, yt, gates, g, g_layer, *, want_x, norm_dtype, split=None):
    T, D = x.shape
    tb = math.gcd(COMBINE_ROWS, T if split is None else math.gcd(*split))
    nblk = T // tb
    row = lambda i: (i, 0)
    out_specs, out_shape = [], []
    if want_x:
        out_specs.append(pl.BlockSpec((tb, D), row))
        out_shape.append(jax.ShapeDtypeStruct((T, D), F32))
    n1 = None
    if split is None:
        out_specs.append(pl.BlockSpec((tb, D), row))
        out_shape.append(jax.ShapeDtypeStruct((T, D), norm_dtype))
    else:
        assert split[0] % tb == 0 and split[1] % tb == 0 and split[0] + split[1] == T
        n1 = split[0] // tb
        out_specs += list(_two_group_specs((tb, D), n1))
        out_shape += [jax.ShapeDtypeStruct((split[0], D), norm_dtype),
                      jax.ShapeDtypeStruct((split[1], D), norm_dtype)]
    return pl.pallas_call(
        functools.partial(_combine_kernel, want_x=want_x, n1=n1),
        grid=(nblk,),
        in_specs=[pl.BlockSpec((tb, D), row),
                  pl.BlockSpec((tb, D), row),
                  pl.BlockSpec((tb, D), lambda i: (i + nblk, 0)),
                  pl.BlockSpec((tb, LANES), row),
                  pl.BlockSpec((None, 1, D), lambda i: (g_layer, 0, 0))],
        out_specs=out_specs,
        out_shape=out_shape,
        compiler_params=_cparams(1),
        name="combine",
    )(x, yt, yt, gates, g)


def moe_layer(x, layer, norm_ffn, w_grp, b_grp, w_exp, b_exp, w_gate, w_up, w_down,
              g_next, g_next_layer, *, want_x, norm_dtype, split=None):
    n_experts = w_exp.shape[2]
    n_route = MOE_GROUPS + n_experts
    assert n_route <= LANES and n_experts == MOE_GROUPS * MOE_PER_GROUP
    w_r = jnp.pad(jnp.concatenate([w_grp[layer], w_exp[layer]], axis=1), ((0, 0), (0, LANES - n_route)))
    b_r = jnp.pad(jnp.concatenate([b_grp[layer], b_exp[layer]]), (0, LANES - n_route)).reshape(1, LANES)
    hp, ids, gates = router(x, norm_ffn, w_r, b_r, layer)
    tables = dispatch_tables(ids[:, :MOE_TOPK], n_experts)
    yt = experts(hp, w_gate, w_up, w_down, layer, *tables)
    return combine(x, yt, gates, g_next, g_next_layer, want_x=want_x, norm_dtype=norm_dtype,
                   split=split)


def kernel(x_prompt, x_sample, state_gla, cache_conv, norm_mix, norm_ffn, norm_final,
           gla_w_in, gla_w_g2, gla_b_g2, gla_norm, gla_w_o,
           conv_w_pw1, conv_b_pw1, conv_w_dw, conv_b_dw, conv_ln_g, conv_ln_b, conv_w_pw2, conv_b_pw2,
           moe_w_grp, moe_b_grp, moe_w_exp, moe_b_exp, moe_w_gate, moe_w_up, moe_w_down):
    Bp, Lp, D = x_prompt.shape
    Bs, Ls, _ = x_sample.shape
    Tp, Ts = Bp * Lp, Bs * Ls
    depth = norm_mix.shape[0]
    assert depth == 2, "layer 0 is the GLA mixer, layer 1 the convolution mixer"
    dk = gla_w_g2.shape[2]
    dv = gla_w_o.shape[1]
    rank = gla_w_g2.shape[1]
    ch = conv_w_dw.shape[2]
    n_qkvr = 2 * dk + 2 * dv
    assert rank <= LANES and n_qkvr % LANES == 0

    def vec3(a):
        return a.reshape(a.shape[0], 1, a.shape[1])

    norm_mix3, norm_ffn3, norm_final3 = vec3(norm_mix), vec3(norm_ffn), norm_final.reshape(1, 1, D)
    x_p, x_s = x_prompt.reshape(Tp, D), x_sample.reshape(Ts, D)
    tm = math.gcd(math.gcd(Tp, Ts), 1024)
    sample_nb = 16 // math.gcd(Ls, 16)

    h = rmsnorm_rows(x_p, x_s, norm_mix3, 0, tm // 2)
    w_in_t = jnp.swapaxes(gla_w_in, 1, 2)
    proj = matmul(h, w_in_t, 0, n_qkvr, tn=512, tm=tm, w_t=True, out_dtype=BF16, name="gla_proj")
    gl = matmul(h, w_in_t, 0, LANES, tn=LANES, tm=tm, w_t=True, col0=n_qkvr, n_valid=rank,
                name="gla_gate_proj")
    w_g2 = jnp.pad(gla_w_g2[0], ((0, LANES - rank), (0, 0)))
    gla_args = (proj, gl, w_g2, vec3(gla_b_g2), vec3(gla_norm), 0)
    o_p, s_prompt = gla_group(*gla_args, row0=0, batch=Bp, seq=Lp, nb=1, dk_total=dk, dv_total=dv,
                              name="gla_prompt")
    o_s, s_sample = gla_group(*gla_args, row0=Tp, batch=Bs, seq=Ls, nb=sample_nb, dk_total=dk,
                              dv_total=dv, s0=state_gla, name="gla_sample")
    x = matmul(o_p, gla_w_o, 0, D, tn=512, tm=tm // 2, a2=o_s, res=x_p, res2=x_s, name="gla_out")
    x, h = moe_layer(x, 0, norm_ffn3, moe_w_grp, moe_b_grp, moe_w_exp, moe_b_exp,
                     moe_w_gate, moe_w_up, moe_w_down, norm_mix3, 1, want_x=True, norm_dtype=BF16)

    u = matmul(h, conv_w_pw1, 0, ch, tn=256, tm=tm, bias=vec3(conv_b_pw1), glu=True, name="conv_pw1")
    conv_args = (u, conv_w_dw, vec3(conv_b_dw), vec3(conv_ln_g), vec3(conv_ln_b), 0)
    z_p, c_prompt = conv_group(*conv_args, row0=0, batch=Bp, seq=Lp, nb=1, tl=math.gcd(Lp, 64),
                               name="conv_prompt")
    z_s, c_sample = conv_group(*conv_args, row0=Tp, batch=Bs, seq=Ls, nb=sample_nb, tl=Ls,
                               cache=cache_conv, name="conv_sample")
    x = matmul(z_p, conv_w_pw2, 0, D, tn=512, tm=tm // 2, a2=z_s, bias=vec3(conv_b_pw2), res=x,
               name="conv_pw2")
    y_p, y_s = moe_layer(x, 1, norm_ffn3, moe_w_grp, moe_b_grp, moe_w_exp, moe_b_exp,
                         moe_w_gate, moe_w_up, moe_w_down, norm_final3, 0, want_x=False,
                         norm_dtype=F32, split=(Tp, Ts))

    return (y_p.reshape(Bp, Lp, D), y_s.reshape(Bs, Ls, D), s_prompt[None], s_sample[None],
            c_prompt[None], c_sample[None])
```

```python
import functools
import math

import jax
import jax.numpy as jnp
from jax import lax
from jax.experimental import pallas as pl
from jax.experimental.pallas import tpu as pltpu

F32 = jnp.float32
BF16 = jnp.bfloat16
U32 = jnp.uint32
I32 = jnp.int32

EPS = 1e-6
GLA_HEADS = 4
GLA_GATE_NORM = 16.0
GLA_CHUNK = 64
CONV_WIDTH = 31
MOE_GROUPS = 8
MOE_PER_GROUP = 8
MOE_TOPK = 2

LANES = 128
SUBLANES = 8
VMEM_LIMIT = 56 * 1024 * 1024
EXPERT_ROWS = 512
EXPERT_SUB = 128
EXPERT_FT = 256
COMBINE_ROWS = 256
DMA_ISSUE_UNROLL = 8


def _cparams(n_axes, **kw):
    return pltpu.CompilerParams(
        dimension_semantics=("arbitrary",) * n_axes, vmem_limit_bytes=VMEM_LIMIT, **kw)


def _nt(a, b):
    return lax.dot_general(a, b, (((1,), (1,)), ((), ())), preferred_element_type=F32)


def _tn(a, b):
    return lax.dot_general(a, b, (((0,), (0,)), ((), ())), preferred_element_type=F32)


def _dot(a, b):
    return jnp.dot(a, b, preferred_element_type=F32)


def _split_bf16(x):
    hi = x.astype(BF16)
    lo = (x - hi.astype(F32)).astype(BF16)
    return hi, lo


def _sigmoid(x):
    return 1.0 / (1.0 + jnp.exp(-x))


def _rms(x, g):
    return x * lax.rsqrt(jnp.mean(x * x, axis=-1, keepdims=True) + EPS) * g


def _two_group_specs(block, n1):
    return (pl.BlockSpec(block, lambda *ix: (jnp.minimum(ix[-1], n1 - 1), 0)),
            pl.BlockSpec(block, lambda *ix: (jnp.maximum(ix[-1] - n1, 0), 0)))


def _rmsnorm_kernel(xp_ref, xs_ref, g_ref, o_ref, *, n1):
    def rows_from(x_ref):
        o_ref[...] = _rms(x_ref[...], g_ref[...]).astype(o_ref.dtype)

    pl.when(pl.program_id(0) < n1)(lambda: rows_from(xp_ref))
    pl.when(pl.program_id(0) >= n1)(lambda: rows_from(xs_ref))


def rmsnorm_rows(x_p, x_s, g, layer, tm):
    D = x_p.shape[1]
    n1 = x_p.shape[0] // tm
    T = x_p.shape[0] + x_s.shape[0]
    return pl.pallas_call(
        functools.partial(_rmsnorm_kernel, n1=n1),
        grid=(T // tm,),
        in_specs=[*_two_group_specs((tm, D), n1),
                  pl.BlockSpec((None, 1, D), lambda i: (layer, 0, 0))],
        out_specs=pl.BlockSpec((tm, D), lambda i: (i, 0)),
        out_shape=jax.ShapeDtypeStruct((T, D), BF16),
        compiler_params=_cparams(1),
        name="rmsnorm",
    )(x_p, x_s, g)


def _mm_kernel(*refs, glu, has_bias, n_res, n1, w_t, n_valid):
    it = iter(refs)
    a_ref = next(it)
    a2_ref = next(it) if n1 is not None else None
    w_ref = next(it)
    w2_ref = next(it) if glu else None
    b_ref = next(it) if has_bias else None
    b2_ref = next(it) if (glu and has_bias) else None
    res_ref = next(it) if n_res >= 1 else None
    res2_ref = next(it) if n_res == 2 else None
    o_ref = next(it)
    wb_ref = next(it)
    wb2_ref = next(it) if glu else None

    @pl.when(pl.program_id(1) == 0)
    def _():
        wb_ref[...] = w_ref[...].astype(BF16)
        if glu:
            wb2_ref[...] = w2_ref[...].astype(BF16)

    mm = _nt if w_t else _dot

    def rows_from(src_ref, r_ref):
        a = src_ref[...]
        acc = mm(a, wb_ref[...])
        if n_valid is not None:
            col = lax.broadcasted_iota(I32, acc.shape, 1)
            acc = jnp.where(col < n_valid, acc, 0.0)
        if has_bias:
            acc = acc + b_ref[...]
        if glu:
            gate = mm(a, wb2_ref[...])
            if has_bias:
                gate = gate + b2_ref[...]
            acc = acc * _sigmoid(gate)
        if r_ref is not None:
            acc = acc + r_ref[...]
        o_ref[...] = acc.astype(o_ref.dtype)

    if n1 is None:
        rows_from(a_ref, res_ref)
    else:
        second = res2_ref if n_res == 2 else res_ref
        pl.when(pl.program_id(1) < n1)(lambda: rows_from(a_ref, res_ref))
        pl.when(pl.program_id(1) >= n1)(lambda: rows_from(a2_ref, second))


def matmul(a, w, layer, n_out, *, tn, tm, a2=None, bias=None, res=None, res2=None, glu=False,
           w_t=False, col0=0, n_valid=None, out_dtype=F32, name="matmul"):
    M, K = a.shape
    n1 = None
    if a2 is None:
        in_specs = [pl.BlockSpec((tm, K), lambda j, i: (i, 0))]
        args = [a]
    else:
        assert M % tm == 0 and a2.shape[0] % tm == 0
        n1 = M // tm
        M = M + a2.shape[0]
        in_specs = list(_two_group_specs((tm, K), n1))
        args = [a, a2]
    assert col0 % tn == 0 and M % tm == 0
    nj, ni = n_out // tn, M // tm
    j0 = col0 // tn
    goff = j0 + n_out // tn
    if w_t:
        wblock = (None, tn, K)
        wmap = lambda off: (lambda j, i: (layer, j + off, 0))
    else:
        wblock = (None, K, tn)
        wmap = lambda off: (lambda j, i: (layer, 0, j + off))
    in_specs.append(pl.BlockSpec(wblock, wmap(j0)))
    args.append(w)
    if glu:
        in_specs.append(pl.BlockSpec(wblock, wmap(goff)))
        args.append(w)
    if bias is not None:
        in_specs.append(pl.BlockSpec((None, 1, tn), lambda j, i: (layer, 0, j + j0)))
        args.append(bias)
        if glu:
            in_specs.append(pl.BlockSpec((None, 1, tn), lambda j, i: (layer, 0, j + goff)))
            args.append(bias)
    n_res = 0
    if res is not None and res2 is None:
        n_res = 1
        in_specs.append(pl.BlockSpec((tm, tn), lambda j, i: (i, j)))
        args.append(res)
    elif res is not None:
        n_res = 2
        in_specs.append(pl.BlockSpec((tm, tn), lambda j, i: (jnp.minimum(i, n1 - 1), j)))
        in_specs.append(pl.BlockSpec((tm, tn), lambda j, i: (jnp.maximum(i - n1, 0), j)))
        args += [res, res2]
    scratch = [pltpu.VMEM(wblock[1:], BF16)] * (2 if glu else 1)
    return pl.pallas_call(
        functools.partial(_mm_kernel, glu=glu, has_bias=bias is not None, n_res=n_res, n1=n1,
                          w_t=w_t, n_valid=n_valid),
        grid=(nj, ni),
        in_specs=in_specs,
        out_specs=pl.BlockSpec((tm, tn), lambda j, i: (i, j)),
        out_shape=jax.ShapeDtypeStruct((M, n_out), out_dtype),
        scratch_shapes=scratch,
        compiler_params=_cparams(2),
        name=name,
    )(*args)


def _gla_kernel(*refs, nb, ns, chunk, has_s0, scale):
    it = iter(refs)
    q_ref, k_ref, v_ref, r_ref, gl_ref, wg2_ref, bg2_ref, ng_ref = [next(it) for _ in range(8)]
    s0_ref = next(it) if has_s0 else None
    o_ref, sout_ref, s_ref = next(it), next(it), next(it)

    c = pl.program_id(2)
    rows = nb * ns * chunk
    dv = v_ref.shape[1]
    shift = chunk.bit_length() - 1

    @pl.when(c == 0)
    def _():
        if has_s0:
            s_ref[...] = s0_ref[...]
        else:
            s_ref[...] = jnp.zeros(s_ref.shape, F32)

    q = q_ref[...].astype(F32) * scale
    k = k_ref[...].astype(F32)
    v = v_ref[...]
    z = _dot(gl_ref[...].astype(BF16), wg2_ref[...].astype(BF16)) + bg2_ref[...]
    logf = (jnp.minimum(z, 0.0) - jnp.log1p(jnp.exp(-jnp.abs(z)))) * (1.0 / GLA_GATE_NORM)

    ri = lax.broadcasted_iota(I32, (rows, rows), 0)
    ci = lax.broadcasted_iota(I32, (rows, rows), 1)
    same = (ri >> shift) == (ci >> shift)
    tril = same & (ci <= ri)
    l_hi, l_lo = _split_bf16(logf)
    tril_b = tril.astype(BF16)
    same_b = same.astype(BF16)
    b = _dot(tril_b, l_hi) + _dot(tril_b, l_lo)
    bl = _dot(same_b, l_hi) + _dot(same_b, l_lo)
    qe = (q * jnp.exp(b)).astype(BF16)
    ke = (k * jnp.exp(-b)).astype(BF16)
    kd = k * jnp.exp(bl - b)
    scores = jnp.where(tril, _nt(qe, ke), 0.0)
    q_state = qe
    if ns > 1:
        earlier = ((ci >> shift) < (ri >> shift))
        later_b = ((ci >> shift) > (ri >> shift)).astype(BF16)
        earlier_b = earlier.astype(BF16)
        before = _dot(earlier_b, l_hi) + _dot(earlier_b, l_lo)
        after = _dot(later_b, l_hi) + _dot(later_b, l_lo)
        scores = scores + jnp.where(earlier, _nt(qe, kd.astype(BF16)), 0.0)
        q_state = (q * jnp.exp(b + before)).astype(BF16)
        kd = kd * jnp.exp(after)
    o = _dot(scores.astype(BF16), v)

    ones = jnp.ones((rows, LANES), BF16)
    rowseq = lax.broadcasted_iota(I32, (rows, 1), 0) >> shift
    for n in range(nb):
        s_old = s_ref[n]
        o_n = _dot(q_state, s_old.astype(BF16))
        if nb > 1:
            mine = rowseq == n
            o = o + jnp.where(mine, o_n, 0.0)
            kd_n = jnp.where(mine, kd, 0.0).astype(BF16)
            lh_n = jnp.where(mine, l_hi, jnp.zeros_like(l_hi))
            ll_n = jnp.where(mine, l_lo, jnp.zeros_like(l_lo))
        else:
            o = o + o_n
            kd_n = kd.astype(BF16)
            lh_n, ll_n = l_hi, l_lo
        decay = jnp.exp(_tn(lh_n, ones) + _tn(ll_n, ones))
        decay = jnp.concatenate([decay] * (dv // LANES), axis=1)
        s_ref[n] = s_old * decay + _tn(kd_n, v)

    o = _rms(o, ng_ref[...])
    r = r_ref[...].astype(F32)
    o_ref[...] = (o * (r * _sigmoid(r))).astype(o_ref.dtype)

    @pl.when(c == pl.num_programs(2) - 1)
    def _():
        sout_ref[...] = s_ref[...]


def gla_group(proj, gl, w_g2, b_g2, norm_g, layer, *, row0, batch, seq, nb, dk_total, dv_total,
              s0=None, name="gla"):
    H = GLA_HEADS
    dkh, dvh = dk_total // H, dv_total // H
    chunk = math.gcd(seq, GLA_CHUNK)
    ns = 2 if (nb == 1 and seq % (2 * chunk) == 0) else 1
    nchunk = seq // (ns * chunk)
    rows = nb * ns * chunk
    assert chunk & (chunk - 1) == 0 and batch % nb == 0 and row0 % rows == 0
    assert nb == 1 or nchunk == 1
    assert (2 * dk_total) % dvh == 0 and rows % 16 == 0
    rb0 = row0 // rows
    voff = 2 * dk_total // dvh
    roff = voff + H

    def rowblk(bb, c):
        return rb0 + bb * nchunk + c

    in_specs = [
        pl.BlockSpec((rows, dkh), lambda bb, h, c: (rowblk(bb, c), h)),
        pl.BlockSpec((rows, dkh), lambda bb, h, c: (rowblk(bb, c), H + h)),
        pl.BlockSpec((rows, dvh), lambda bb, h, c: (rowblk(bb, c), voff + h)),
        pl.BlockSpec((rows, dvh), lambda bb, h, c: (rowblk(bb, c), roff + h)),
        pl.BlockSpec((rows, LANES), lambda bb, h, c: (rowblk(bb, c), 0)),
        pl.BlockSpec((LANES, dkh), lambda bb, h, c: (0, h)),
        pl.BlockSpec((None, 1, dkh), lambda bb, h, c: (layer, 0, h)),
        pl.BlockSpec((None, 1, dvh), lambda bb, h, c: (layer, 0, 0)),
    ]
    args = [proj, proj, proj, proj, gl, w_g2, b_g2, norm_g]
    if s0 is not None:
        in_specs.append(pl.BlockSpec((None, nb, None, dkh, dvh), lambda bb, h, c: (layer, bb, h, 0, 0)))
        args.append(s0)
    kern = functools.partial(_gla_kernel, nb=nb, ns=ns, chunk=chunk, has_s0=s0 is not None,
                             scale=float(dkh) ** -0.5)
    return pl.pallas_call(
        kern,
        grid=(batch // nb, H, nchunk),
        in_specs=in_specs,
        out_specs=[pl.BlockSpec((rows, dvh), lambda bb, h, c: (bb * nchunk + c, h)),
                   pl.BlockSpec((nb, None, dkh, dvh), lambda bb, h, c: (bb, h, 0, 0))],
        out_shape=[jax.ShapeDtypeStruct((batch * seq, dv_total), BF16),
                   jax.ShapeDtypeStruct((batch, H, dkh, dvh), F32)],
        scratch_shapes=[pltpu.VMEM((nb, dkh, dvh), F32)],
        compiler_params=_cparams(3),
        name=name,
    )(*args)


CONV_HALO = CONV_WIDTH - 1
CONV_PAD = -(-CONV_HALO // SUBLANES) * SUBLANES
CONV_SHIFT_ROWS = (CONV_PAD // SUBLANES - 1) * SUBLANES


def _conv_kernel(*refs, nb, tl, has_cache):
    it = iter(refs)
    u_ref, wdw_ref, bdw_ref, lng_ref, lnb_ref = [next(it) for _ in range(5)]
    cache_ref = next(it) if has_cache else None
    z_ref, cout_ref, win_ref, sh_ref, acc_ref = [next(it) for _ in range(5)]

    t = pl.program_id(1)
    ch = u_ref.shape[1]
    first = CONV_PAD - CONV_HALO

    for n in range(nb):
        @pl.when(t == 0)
        def _():
            win_ref[n, 0:CONV_PAD, :] = jnp.zeros((CONV_PAD, ch), F32)
            if has_cache:
                win_ref[n, first:CONV_PAD, :] = cache_ref[n]

        @pl.when(t > 0)
        def _():
            win_ref[n, 0:CONV_PAD, :] = win_ref[n, tl:tl + CONV_PAD, :]

        win_ref[n, CONV_PAD:CONV_PAD + tl, :] = u_ref[n * tl:(n + 1) * tl, :]

        for p in range(1, SUBLANES):
            sh_ref[p - 1] = win_ref[n, p:p + tl + CONV_SHIFT_ROWS, :]

        for cb in range(ch // LANES):
            lanes = slice(cb * LANES, (cb + 1) * LANES)
            acc = jnp.zeros((tl, LANES), F32) + bdw_ref[:, lanes]
            for w in range(CONV_WIDTH):
                off = first + w
                a, p = off // SUBLANES * SUBLANES, off % SUBLANES
                rows = win_ref[n, a:a + tl, lanes] if p == 0 else sh_ref[p - 1, a:a + tl, lanes]
                acc = acc + rows * wdw_ref[w:w + 1, lanes]
            acc_ref[n * tl:(n + 1) * tl, lanes] = acc

        @pl.when(t == pl.num_programs(1) - 1)
        def _():
            cout_ref[n] = win_ref[n, CONV_PAD + tl - CONV_HALO:CONV_PAD + tl, :]

    zc = acc_ref[...]
    mu = jnp.mean(zc, axis=-1, keepdims=True)
    d = zc - mu
    var = jnp.mean(d * d, axis=-1, keepdims=True)
    y = d * lax.rsqrt(var + EPS) * lng_ref[...] + lnb_ref[...]
    z_ref[...] = (y * _sigmoid(y)).astype(z_ref.dtype)


def conv_group(u, w_dw, b_dw, ln_g, ln_b, layer, *, row0, batch, seq, nb, tl, cache=None,
               name="conv"):
    ch = u.shape[1]
    nt = seq // tl
    rows = nb * tl
    assert seq % tl == 0 and batch % nb == 0 and row0 % rows == 0 and rows % 16 == 0
    assert (nb == 1 or nt == 1) and tl % SUBLANES == 0
    rb0 = row0 // rows

    in_specs = [
        pl.BlockSpec((rows, ch), lambda bb, t: (rb0 + bb * nt + t, 0)),
        pl.BlockSpec((None, CONV_WIDTH, ch), lambda bb, t: (layer, 0, 0)),
        pl.BlockSpec((None, 1, ch), lambda bb, t: (layer, 0, 0)),
        pl.BlockSpec((None, 1, ch), lambda bb, t: (layer, 0, 0)),
        pl.BlockSpec((None, 1, ch), lambda bb, t: (layer, 0, 0)),
    ]
    args = [u, w_dw, b_dw, ln_g, ln_b]
    if cache is not None:
        in_specs.append(pl.BlockSpec((None, nb, CONV_HALO, ch), lambda bb, t: (layer, bb, 0, 0)))
        args.append(cache)
    kern = functools.partial(_conv_kernel, nb=nb, tl=tl, has_cache=cache is not None)
    return pl.pallas_call(
        kern,
        grid=(batch // nb, nt),
        in_specs=in_specs,
        out_specs=[pl.BlockSpec((rows, ch), lambda bb, t: (bb * nt + t, 0)),
                   pl.BlockSpec((nb, CONV_HALO, ch), lambda bb, t: (bb, 0, 0))],
        out_shape=[jax.ShapeDtypeStruct((batch * seq, ch), BF16),
                   jax.ShapeDtypeStruct((batch, CONV_HALO, ch), F32)],
        scratch_shapes=[pltpu.VMEM((nb, CONV_PAD + tl, ch), F32),
                        pltpu.VMEM((SUBLANES - 1, tl + CONV_SHIFT_ROWS, ch), F32),
                        pltpu.VMEM((rows, ch), F32)],
        compiler_params=_cparams(2),
        name=name,
    )(*args)


def _router_kernel(x_ref, g_ref, wr_ref, br_ref, hp_ref, ids_ref, gates_ref):
    h = _rms(x_ref[...], g_ref[...])
    half = h.shape[1] // 2

    bits = lax.bitcast_convert_type(h.astype(BF16).astype(F32), U32)
    hp_ref[...] = (bits[:, half:] & jnp.uint32(0xFFFF0000)) | (bits[:, :half] >> 16)

    h_hi, h_lo = _split_bf16(h)
    w_hi, w_lo = _split_bf16(wr_ref[...])
    logits = _dot(h_hi, w_hi) + _dot(h_lo, w_hi) + _dot(h_hi, w_lo) + br_ref[...]

    lane = lax.broadcasted_iota(I32, logits.shape, 1).astype(F32)
    neg = jnp.float32(-jnp.inf)
    big = jnp.float32(4 * LANES)
    is_grp = lane < MOE_GROUPS
    gl = jnp.where(is_grp, logits, neg)
    gmax = jnp.max(gl, axis=-1, keepdims=True)
    g_sel = jnp.min(jnp.where(is_grp & (gl == gmax), lane, big), axis=-1, keepdims=True)
    p_g = 1.0 / jnp.sum(jnp.where(is_grp, jnp.exp(gl - gmax), 0.0), axis=-1, keepdims=True)

    lo = MOE_GROUPS + g_sel * MOE_PER_GROUP
    in_grp = (lane >= lo) & (lane < lo + MOE_PER_GROUP)
    el = jnp.where(in_grp, logits, neg)
    v1 = jnp.max(el, axis=-1, keepdims=True)
    j1 = jnp.min(jnp.where(in_grp & (el == v1), lane, big), axis=-1, keepdims=True)
    el2 = jnp.where(lane == j1, neg, el)
    v2 = jnp.max(el2, axis=-1, keepdims=True)
    j2 = jnp.min(jnp.where(in_grp & (lane != j1) & (el2 == v2), lane, big), axis=-1, keepdims=True)
    e2 = jnp.exp(v2 - v1)
    w1 = p_g / (1.0 + e2)
    w2 = p_g * e2 / (1.0 + e2)

    first = lane == 0.0
    ids_ref[...] = jnp.where(first, j1, j2).astype(I32) - MOE_GROUPS
    gates_ref[...] = jnp.where(first, w1, w2)


def router(x, g, w_r, b_r, layer, tm=256):
    T, D = x.shape
    tm = math.gcd(T, tm)
    return pl.pallas_call(
        _router_kernel,
        grid=(T // tm,),
        in_specs=[pl.BlockSpec((tm, D), lambda i: (i, 0)),
                  pl.BlockSpec((None, 1, D), lambda i: (layer, 0, 0)),
                  pl.BlockSpec((D, LANES), lambda i: (0, 0)),
                  pl.BlockSpec((1, LANES), lambda i: (0, 0))],
        out_specs=[pl.BlockSpec((tm, D // 2), lambda i: (i, 0)),
                   pl.BlockSpec((tm, LANES), lambda i: (i, 0)),
                   pl.BlockSpec((tm, LANES), lambda i: (i, 0))],
        out_shape=[jax.ShapeDtypeStruct((T, D // 2), U32),
                   jax.ShapeDtypeStruct((T, LANES), I32),
                   jax.ShapeDtypeStruct((T, LANES), F32)],
        compiler_params=_cparams(1),
        name="router",
    )(x, g, w_r, b_r)


def dispatch_tables(ids, n_experts):
    T, K = ids.shape
    A = T * K
    nc_max = A // EXPERT_ROWS + n_experts + 1
    flat_e = ids.reshape(-1)
    order = jnp.argsort(flat_e).astype(I32)
    counts = jnp.sum((flat_e[:, None] == jnp.arange(n_experts, dtype=I32)[None, :]).astype(I32), axis=0)
    seg_start = jnp.cumsum(counts) - counts
    nch = (counts + EXPERT_ROWS - 1) // EXPERT_ROWS
    ch_end = jnp.cumsum(nch)
    ch_start = ch_end - nch
    n_used = ch_end[-1]
    cidx = jnp.arange(nc_max, dtype=I32)
    c_eff = jnp.minimum(cidx, n_used - 1)
    chunk_e = jnp.minimum(jnp.sum((ch_end[None, :] <= c_eff[:, None]).astype(I32), axis=1), n_experts - 1)
    sel = (chunk_e[:, None] == jnp.arange(n_experts, dtype=I32)[None, :]).astype(I32)
    k_in_e = c_eff - jnp.sum(sel * ch_start[None, :], axis=1)
    cnt_e = jnp.sum(sel * counts[None, :], axis=1)
    nrows = jnp.where(cidx < n_used, jnp.clip(cnt_e - k_in_e * EXPERT_ROWS, 0, EXPERT_ROWS), 0)
    base = jnp.sum(sel * seg_start[None, :], axis=1) + k_in_e * EXPERT_ROWS
    return (chunk_e.astype(I32), nrows.astype(I32), base.astype(I32),
            n_used.reshape(1).astype(I32), order)


def _issue_rows(n_real, real, mixed):
    @pl.when(n_real == EXPERT_SUB)
    def _():
        lax.fori_loop(0, EXPERT_SUB, real, 0, unroll=DMA_ISSUE_UNROLL)

    @pl.when(n_real < EXPERT_SUB)
    def _():
        lax.fori_loop(0, EXPERT_SUB, mixed, 0, unroll=DMA_ISSUE_UNROLL)


def _expert_kernel(ce_ref, nrows_ref, base_ref, nused_ref, order_ref, hp_hbm, wg_ref, wu_ref,
                   wd_ref, yt_hbm, xraw_ref, x_ref, y_ref, gsem, ssem, *, n_tok, n_assign, nft):
    c = pl.program_id(0)
    s = pl.program_id(1)
    nused = nused_ref[0]
    nsub = (nrows_ref[c] + EXPERT_SUB - 1) // EXPERT_SUB
    half = xraw_ref.shape[1]
    n_subs = EXPERT_ROWS // EXPERT_SUB
    slot = c % 2

    def sub_rows(cc, sub):
        return (base_ref[cc] + sub * EXPERT_SUB,
                jnp.clip(nrows_ref[cc] - sub * EXPERT_SUB, 0, EXPERT_SUB))

    def issue_gather(cc, sub):
        @pl.when(sub * EXPERT_SUB < nrows_ref[cc])
        def _():
            first, n_real = sub_rows(cc, sub)
            tok0 = order_ref[base_ref[cc]] >> 1

            def dst(r):
                return xraw_ref.at[pl.ds(sub * EXPERT_SUB + r, 1), :]

            def real(r, carry):
                tok = order_ref[first + r] >> 1
                pltpu.make_async_copy(hp_hbm.at[pl.ds(tok, 1), :], dst(r), gsem.at[sub]).start()
                return carry

            def mixed(r, carry):
                tok = jnp.where(r < n_real, order_ref[first + jnp.minimum(r, n_real - 1)] >> 1, tok0)
                pltpu.make_async_copy(hp_hbm.at[pl.ds(tok, 1), :], dst(r), gsem.at[sub]).start()
                return carry

            _issue_rows(n_real, real, mixed)

    def issue_scatter(cc, sl, sub):
        @pl.when(sub * EXPERT_SUB < nrows_ref[cc])
        def _():
            first, n_real = sub_rows(cc, sub)
            dump = n_assign + cc * EXPERT_SUB

            def src(r):
                return y_ref.at[sl, pl.ds(sub * EXPERT_SUB + r, 1), :]

            def real(r, carry):
                a = order_ref[first + r]
                row = (a & 1) * n_tok + (a >> 1)
                pltpu.make_async_copy(src(r), yt_hbm.at[pl.ds(row, 1), :], ssem.at[sl, sub]).start()
                return carry

            def mixed(r, carry):
                a = order_ref[first + jnp.minimum(r, n_real - 1)]
                row = jnp.where(r < n_real, (a & 1) * n_tok + (a >> 1), dump + r)
                pltpu.make_async_copy(src(r), yt_hbm.at[pl.ds(row, 1), :], ssem.at[sl, sub]).start()
                return carry

            _issue_rows(n_real, real, mixed)

    def wait_scatter(cc, sl):
        for sub in range(n_subs):
            @pl.when(sub * EXPERT_SUB < nrows_ref[cc])
            def _():
                rows = pl.ds(sub * EXPERT_SUB, EXPERT_SUB)
                pltpu.make_async_copy(y_ref.at[sl, rows, :], yt_hbm.at[rows, :], ssem.at[sl, sub]).wait()

    @pl.when((c == 0) & (s == 0))
    def _():
        for sub in range(n_subs):
            issue_gather(0, sub)

    @pl.when((s == 0) & (c < nused))
    def _():
        for sub in range(n_subs):
            @pl.when(sub < nsub)
            def _():
                rows = pl.ds(sub * EXPERT_SUB, EXPERT_SUB)
                pltpu.make_async_copy(hp_hbm.at[rows, :], xraw_ref.at[rows, :], gsem.at[sub]).wait()
                xu = xraw_ref[rows, :]
                x_ref[rows, 0:half] = lax.bitcast_convert_type(xu << 16, F32).astype(BF16)
                x_ref[rows, half:2 * half] = lax.bitcast_convert_type(
                    xu & jnp.uint32(0xFFFF0000), F32).astype(BF16)

    for sub in range(n_subs):
        @pl.when(s == sub % nft)
        def _():
            @pl.when(c + 1 < nused)
            def _():
                issue_gather(c + 1, sub)

            @pl.when((c >= 1) & (c <= nused))
            def _():
                issue_scatter(c - 1, 1 - slot, sub)

    @pl.when(c < nused)
    def _():
        for m in range(1, n_subs + 1):
            @pl.when(nsub == m)
            def _():
                rows = slice(0, m * EXPERT_SUB)
                x = x_ref[rows, :]
                g = _dot(x, wg_ref[...].astype(BF16))
                u = _dot(x, wu_ref[...].astype(BF16))
                hcol = (g * _sigmoid(g) * u).astype(BF16)
                part = _dot(hcol, wd_ref[...].astype(BF16))

                @pl.when(s == 0)
                def _():
                    @pl.when(c >= 2)
                    def _():
                        wait_scatter(c - 2, slot)
                    y_ref[slot, rows, :] = part

                @pl.when(s > 0)
                def _():
                    y_ref[slot, rows, :] = y_ref[slot, rows, :] + part

    @pl.when((c == pl.num_programs(0) - 1) & (s == nft - 1))
    def _():
        wait_scatter(nused - 1, (nused - 1) % 2)

        @pl.when(nused >= 2)
        def _():
            wait_scatter(nused - 2, nused % 2)


def experts(hp, w_gate, w_up, w_down, layer, chunk_e, nrows, base, n_used, order):
    T = hp.shape[0]
    D, ff = w_gate.shape[2], w_gate.shape[3]
    nc = chunk_e.shape[0]
    A = order.shape[0]
    nft = ff // EXPERT_FT
    assert ff % EXPERT_FT == 0 and A == T * MOE_TOPK and MOE_TOPK == 2

    def f_eff(c, s, nu):
        return jnp.where(c < nu[0], s, nft - 1)

    def gate_map(c, s, ce, nr, ba, nu, od):
        return (layer, ce[c], 0, f_eff(c, s, nu))

    def down_map(c, s, ce, nr, ba, nu, od):
        return (layer, ce[c], f_eff(c, s, nu), 0)

    grid_spec = pltpu.PrefetchScalarGridSpec(
        num_scalar_prefetch=5,
        grid=(nc, nft),
        in_specs=[pl.BlockSpec(memory_space=pl.ANY),
                  pl.BlockSpec((None, None, D, EXPERT_FT), gate_map),
                  pl.BlockSpec((None, None, D, EXPERT_FT), gate_map),
                  pl.BlockSpec((None, None, EXPERT_FT, D), down_map)],
        out_specs=pl.BlockSpec(memory_space=pl.ANY),
        scratch_shapes=[pltpu.VMEM((EXPERT_ROWS, D // 2), U32),
                        pltpu.VMEM((EXPERT_ROWS, D), BF16),
                        pltpu.VMEM((2, EXPERT_ROWS, D), F32),
                        pltpu.SemaphoreType.DMA((EXPERT_ROWS // EXPERT_SUB,)),
                        pltpu.SemaphoreType.DMA((2, EXPERT_ROWS // EXPERT_SUB))],
    )
    return pl.pallas_call(
        functools.partial(_expert_kernel, n_tok=T, n_assign=A, nft=nft),
        grid_spec=grid_spec,
        out_shape=jax.ShapeDtypeStruct((A + nc * EXPERT_SUB, D), F32),
        compiler_params=_cparams(2),
        name="experts",
    )(chunk_e, nrows, base, n_used, order, hp, w_gate, w_up, w_down)


def _combine_kernel(x_ref, y0_ref, y1_ref, gates_ref, g_ref, *out_refs, want_x, n1):
    gates = gates_ref[...]
    x = x_ref[...] + y0_ref[...] * gates[:, 0:1] + y1_ref[...] * gates[:, 1:2]
    it = iter(out_refs)
    if want_x:
        next(it)[...] = x
    y = _rms(x, g_ref[...])
    if n1 is None:
        o_ref = next(it)
        o_ref[...] = y.astype(o_ref.dtype)
    else:
        op_ref, os_ref = next(it), next(it)

        @pl.when(pl.program_id(0) < n1)
        def _():
            op_ref[...] = y.astype(op_ref.dtype)

        @pl.when(pl.program_id(0) >= n1)
        def _():
            os_ref[...] = y.astype(os_ref.dtype)


def combine(x, yt, gates, g, g_layer, *, want_x, norm_dtype, split=None):
    T, D = x.shape
    tb = math.gcd(COMBINE_ROWS, T if split is None else math.gcd(*split))
    nblk = T // tb
    row = lambda i: (i, 0)
    out_specs, out_shape = [], []
    if want_x:
        out_specs.append(pl.BlockSpec((tb, D), row))
        out_shape.append(jax.ShapeDtypeStruct((T, D), F32))
    n1 = None
    if split is None:
        out_specs.append(pl.BlockSpec((tb, D), row))
        out_shape.append(jax.ShapeDtypeStruct((T, D), norm_dtype))
    else:
        assert split[0] % tb == 0 and split[1] % tb == 0 and split[0] + split[1] == T
        n1 = split[0] // tb
        out_specs += list(_two_group_specs((tb, D), n1))
        out_shape += [jax.ShapeDtypeStruct((split[0], D), norm_dtype),
                      jax.ShapeDtypeStruct((split[1], D), norm_dtype)]
    return pl.pallas_call(
        functools.partial(_combine_kernel, want_x=want_x, n1=n1),
        grid=(nblk,),
        in_specs=[pl.BlockSpec((tb, D), row),
                  pl.BlockSpec((tb, D), row),
                  pl.BlockSpec((tb, D), lambda i: (i + nblk, 0)),
                  pl.BlockSpec((tb, LANES), row),
                  pl.BlockSpec((None, 1, D), lambda i: (g_layer, 0, 0))],
        out_specs=out_specs,
        out_shape=out_shape,
        compiler_params=_cparams(1),
        name="combine",
    )(x, yt, yt, gates, g)


def moe_layer(x, layer, norm_ffn, w_grp, b_grp, w_exp, b_exp, w_gate, w_up, w_down,
              g_next, g_next_layer, *, want_x, norm_dtype, split=None):
    n_experts = w_exp.shape[2]
    n_route = MOE_GROUPS + n_experts
    assert n_route <= LANES and n_experts == MOE_GROUPS * MOE_PER_GROUP
    w_r = jnp.pad(jnp.concatenate([w_grp[layer], w_exp[layer]], axis=1), ((0, 0), (0, LANES - n_route)))
    b_r = jnp.pad(jnp.concatenate([b_grp[layer], b_exp[layer]]), (0, LANES - n_route)).reshape(1, LANES)
    hp, ids, gates = router(x, norm_ffn, w_r, b_r, layer)
    tables = dispatch_tables(ids[:, :MOE_TOPK], n_experts)
    yt = experts(hp, w_gate, w_up, w_down, layer, *tables)
    return combine(x, yt, gates, g_next, g_next_layer, want_x=want_x, norm_dtype=norm_dtype,
                   split=split)


def kernel(x_prompt, x_sample, state_gla, cache_conv, norm_mix, norm_ffn, norm_final,
           gla_w_in, gla_w_g2, gla_b_g2, gla_norm, gla_w_o,
           conv_w_pw1, conv_b_pw1, conv_w_dw, conv_b_dw, conv_ln_g, conv_ln_b, conv_w_pw2, conv_b_pw2,
           moe_w_grp, moe_b_grp, moe_w_exp, moe_b_exp, moe_w_gate, moe_w_up, moe_w_down):
    Bp, Lp, D = x_prompt.shape
    Bs, Ls, _ = x_sample.shape
    Tp, Ts = Bp * Lp, Bs * Ls
    depth = norm_mix.shape[0]
    assert depth == 2, "layer 0 is the GLA mixer, layer 1 the convolution mixer"
    dk = gla_w_g2.shape[2]
    dv = gla_w_o.shape[1]
    rank = gla_w_g2.shape[1]
    ch = conv_w_dw.shape[2]
    n_qkvr = 2 * dk + 2 * dv
    assert rank <= LANES and n_qkvr % LANES == 0

    def vec3(a):
        return a.reshape(a.shape[0], 1, a.shape[1])

    norm_mix3, norm_ffn3, norm_final3 = vec3(norm_mix), vec3(norm_ffn), norm_final.reshape(1, 1, D)
    x_p, x_s = x_prompt.reshape(Tp, D), x_sample.reshape(Ts, D)
    tm = math.gcd(math.gcd(Tp, Ts), 1024)
    sample_nb = 16 // math.gcd(Ls, 16)

    h = rmsnorm_rows(x_p, x_s, norm_mix3, 0, tm // 2)
    w_in_t = jnp.swapaxes(gla_w_in, 1, 2)
    proj = matmul(h, w_in_t, 0, n_qkvr, tn=512, tm=tm, w_t=True, out_dtype=BF16, name="gla_proj")
    gl = matmul(h, w_in_t, 0, LANES, tn=LANES, tm=tm, w_t=True, col0=n_qkvr, n_valid=rank,
                name="gla_gate_proj")
    w_g2 = jnp.pad(gla_w_g2[0], ((0, LANES - rank), (0, 0)))
    gla_args = (proj, gl, w_g2, vec3(gla_b_g2), vec3(gla_norm), 0)
    o_p, s_prompt = gla_group(*gla_args, row0=0, batch=Bp, seq=Lp, nb=1, dk_total=dk, dv_total=dv,
                              name="gla_prompt")
    o_s, s_sample = gla_group(*gla_args, row0=Tp, batch=Bs, seq=Ls, nb=sample_nb, dk_total=dk,
                              dv_total=dv, s0=state_gla, name="gla_sample")
    x = matmul(o_p, gla_w_o, 0, D, tn=512, tm=tm // 2, a2=o_s, res=x_p, res2=x_s, name="gla_out")
    x, h = moe_layer(x, 0, norm_ffn3, moe_w_grp, moe_b_grp, moe_w_exp, moe_b_exp,
                     moe_w_gate, moe_w_up, moe_w_down, norm_mix3, 1, want_x=True, norm_dtype=BF16)

    u = matmul(h, conv_w_pw1, 0, ch, tn=256, tm=tm, bias=vec3(conv_b_pw1), glu=True, name="conv_pw1")
    conv_args = (u, conv_w_dw, vec3(conv_b_dw), vec3(conv_ln_g), vec3(conv_ln_b), 0)
    z_p, c_prompt = conv_group(*conv_args, row0=0, batch=Bp, seq=Lp, nb=1, tl=math.gcd(Lp, 64),
                               name="conv_prompt")
    z_s, c_sample = conv_group(*conv_args, row0=Tp, batch=Bs, seq=Ls, nb=sample_nb, tl=Ls,
                               cache=cache_conv, name="conv_sample")
    x = matmul(z_p, conv_w_pw2, 0, D, tn=512, tm=tm // 2, a2=z_s, bias=vec3(conv_b_pw2), res=x,
               name="conv_pw2")
    y_p, y_s = moe_layer(x, 1, norm_ffn3, moe_w_grp, moe_b_grp, moe_w_exp, moe_b_exp,
                         moe_w_gate, moe_w_up, moe_w_down, norm_final3, 0, want_x=False,
                         norm_dtype=F32, split=(Tp, Ts))

    return (y_p.reshape(Bp, Lp, D), y_s.reshape(Bs, Ls, D), s_prompt[None], s_sample[None],
            c_prompt[None], c_sample[None])
```

```python
import functools
import math

import jax
import jax.numpy as jnp
from jax import lax
from jax.experimental import pallas as pl
from jax.experimental.pallas import tpu as pltpu

F32 = jnp.float32
BF16 = jnp.bfloat16
U32 = jnp.uint32
I32 = jnp.int32

EPS = 1e-6
GLA_HEADS = 4
GLA_GATE_NORM = 16.0
GLA_CHUNK = 64
CONV_WIDTH = 31
MOE_GROUPS = 8
MOE_PER_GROUP = 8
MOE_TOPK = 2

LANES = 128
SUBLANES = 8
VMEM_LIMIT = 56 * 1024 * 1024
EXPERT_ROWS = 512
EXPERT_SUB = 128
EXPERT_FT = 256
COMBINE_ROWS = 256
DMA_ISSUE_UNROLL = 8


def _cparams(n_axes, **kw):
    return pltpu.CompilerParams(
        dimension_semantics=("arbitrary",) * n_axes, vmem_limit_bytes=VMEM_LIMIT, **kw)


def _nt(a, b):
    return lax.dot_general(a, b, (((1,), (1,)), ((), ())), preferred_element_type=F32)


def _tn(a, b):
    return lax.dot_general(a, b, (((0,), (0,)), ((), ())), preferred_element_type=F32)


def _dot(a, b):
    return jnp.dot(a, b, preferred_element_type=F32)


def _split_bf16(x):
    hi = x.astype(BF16)
    lo = (x - hi.astype(F32)).astype(BF16)
    return hi, lo


def _sigmoid(x):
    return 1.0 / (1.0 + jnp.exp(-x))


def _pack_bf16_pairs(x):
    half = x.shape[1] // 2
    bits = lax.bitcast_convert_type(x.astype(BF16).astype(F32), U32)
    return (bits[:, half:] & jnp.uint32(0xFFFF0000)) | (bits[:, :half] >> 16)


def _unpack_bf16_pairs(u):
    return (lax.bitcast_convert_type(u << 16, F32),
            lax.bitcast_convert_type(u & jnp.uint32(0xFFFF0000), F32))


def _rms(x, g):
    return x * lax.rsqrt(jnp.mean(x * x, axis=-1, keepdims=True) + EPS) * g


def _two_group_specs(block, n1):
    return (pl.BlockSpec(block, lambda *ix: (jnp.minimum(ix[-1], n1 - 1), 0)),
            pl.BlockSpec(block, lambda *ix: (jnp.maximum(ix[-1] - n1, 0), 0)))


def _rmsnorm_kernel(xp_ref, xs_ref, g_ref, o_ref, *, n1):
    def rows_from(x_ref):
        o_ref[...] = _rms(x_ref[...], g_ref[...]).astype(o_ref.dtype)

    pl.when(pl.program_id(0) < n1)(lambda: rows_from(xp_ref))
    pl.when(pl.program_id(0) >= n1)(lambda: rows_from(xs_ref))


def rmsnorm_rows(x_p, x_s, g, layer, tm):
    D = x_p.shape[1]
    n1 = x_p.shape[0] // tm
    T = x_p.shape[0] + x_s.shape[0]
    return pl.pallas_call(
        functools.partial(_rmsnorm_kernel, n1=n1),
        grid=(T // tm,),
        in_specs=[*_two_group_specs((tm, D), n1),
                  pl.BlockSpec((None, 1, D), lambda i: (layer, 0, 0))],
        out_specs=pl.BlockSpec((tm, D), lambda i: (i, 0)),
        out_shape=jax.ShapeDtypeStruct((T, D), BF16),
        compiler_params=_cparams(1),
        name="rmsnorm",
    )(x_p, x_s, g)


def _mm_kernel(*refs, glu, has_bias, n_res, n1, w_t, n_valid):
    it = iter(refs)
    a_ref = next(it)
    a2_ref = next(it) if n1 is not None else None
    w_ref = next(it)
    w2_ref = next(it) if glu else None
    b_ref = next(it) if has_bias else None
    b2_ref = next(it) if (glu and has_bias) else None
    res_ref = next(it) if n_res >= 1 else None
    res2_ref = next(it) if n_res == 2 else None
    o_ref = next(it)
    wb_ref = next(it)
    wb2_ref = next(it) if glu else None

    @pl.when(pl.program_id(1) == 0)
    def _():
        wb_ref[...] = w_ref[...].astype(BF16)
        if glu:
            wb2_ref[...] = w2_ref[...].astype(BF16)

    mm = _nt if w_t else _dot

    def rows_from(src_ref, r_ref):
        a = src_ref[...]
        acc = mm(a, wb_ref[...])
        if n_valid is not None:
            col = lax.broadcasted_iota(I32, acc.shape, 1)
            acc = jnp.where(col < n_valid, acc, 0.0)
        if has_bias:
            acc = acc + b_ref[...]
        if glu:
            gate = mm(a, wb2_ref[...])
            if has_bias:
                gate = gate + b2_ref[...]
            acc = acc * _sigmoid(gate)
        if r_ref is not None:
            acc = acc + r_ref[...]
        o_ref[...] = acc.astype(o_ref.dtype)

    if n1 is None:
        rows_from(a_ref, res_ref)
    else:
        second = res2_ref if n_res == 2 else res_ref
        pl.when(pl.program_id(1) < n1)(lambda: rows_from(a_ref, res_ref))
        pl.when(pl.program_id(1) >= n1)(lambda: rows_from(a2_ref, second))


def matmul(a, w, layer, n_out, *, tn, tm, a2=None, bias=None, res=None, res2=None, glu=False,
           w_t=False, col0=0, n_valid=None, w_buffers=2, out_dtype=F32, name="matmul"):
    M, K = a.shape
    n1 = None
    if a2 is None:
        in_specs = [pl.BlockSpec((tm, K), lambda j, i: (i, 0))]
        args = [a]
    else:
        assert M % tm == 0 and a2.shape[0] % tm == 0
        n1 = M // tm
        M = M + a2.shape[0]
        in_specs = list(_two_group_specs((tm, K), n1))
        args = [a, a2]
    assert col0 % tn == 0 and M % tm == 0
    nj, ni = n_out // tn, M // tm
    j0 = col0 // tn
    goff = j0 + n_out // tn
    if w_t:
        wblock = (None, tn, K)
        wmap = lambda off: (lambda j, i: (layer, j + off, 0))
    else:
        wblock = (None, K, tn)
        wmap = lambda off: (lambda j, i: (layer, 0, j + off))
    wmode = {} if w_buffers == 2 else {"pipeline_mode": pl.Buffered(w_buffers)}
    in_specs.append(pl.BlockSpec(wblock, wmap(j0), **wmode))
    args.append(w)
    if glu:
        in_specs.append(pl.BlockSpec(wblock, wmap(goff), **wmode))
        args.append(w)
    if bias is not None:
        in_specs.append(pl.BlockSpec((None, 1, tn), lambda j, i: (layer, 0, j + j0)))
        args.append(bias)
        if glu:
            in_specs.append(pl.BlockSpec((None, 1, tn), lambda j, i: (layer, 0, j + goff)))
            args.append(bias)
    n_res = 0
    if res is not None and res2 is None:
        n_res = 1
        in_specs.append(pl.BlockSpec((tm, tn), lambda j, i: (i, j)))
        args.append(res)
    elif res is not None:
        n_res = 2
        in_specs.append(pl.BlockSpec((tm, tn), lambda j, i: (jnp.minimum(i, n1 - 1), j)))
        in_specs.append(pl.BlockSpec((tm, tn), lambda j, i: (jnp.maximum(i - n1, 0), j)))
        args += [res, res2]
    scratch = [pltpu.VMEM(wblock[1:], BF16)] * (2 if glu else 1)
    return pl.pallas_call(
        functools.partial(_mm_kernel, glu=glu, has_bias=bias is not None, n_res=n_res, n1=n1,
                          w_t=w_t, n_valid=n_valid),
        grid=(nj, ni),
        in_specs=in_specs,
        out_specs=pl.BlockSpec((tm, tn), lambda j, i: (i, j)),
        out_shape=jax.ShapeDtypeStruct((M, n_out), out_dtype),
        scratch_shapes=scratch,
        compiler_params=_cparams(2),
        name=name,
    )(*args)


def _gla_kernel(*refs, nb, ns, chunk, has_s0, scale):
    it = iter(refs)
    q_ref, k_ref, v_ref, r_ref, gl_ref, wg2_ref, bg2_ref, ng_ref = [next(it) for _ in range(8)]
    s0_ref = next(it) if has_s0 else None
    o_ref, sout_ref, s_ref = next(it), next(it), next(it)

    c = pl.program_id(2)
    rows = nb * ns * chunk
    dv = v_ref.shape[1]
    shift = chunk.bit_length() - 1

    @pl.when(c == 0)
    def _():
        if has_s0:
            s_ref[...] = s0_ref[...]
        else:
            s_ref[...] = jnp.zeros(s_ref.shape, F32)

    q = q_ref[...].astype(F32) * scale
    k = k_ref[...].astype(F32)
    v = v_ref[...]
    z = _dot(gl_ref[...].astype(BF16), wg2_ref[...].astype(BF16)) + bg2_ref[...]
    logf = (jnp.minimum(z, 0.0) - jnp.log1p(jnp.exp(-jnp.abs(z)))) * (1.0 / GLA_GATE_NORM)

    ri = lax.broadcasted_iota(I32, (rows, rows), 0)
    ci = lax.broadcasted_iota(I32, (rows, rows), 1)
    same = (ri >> shift) == (ci >> shift)
    tril = same & (ci <= ri)
    l_hi, l_lo = _split_bf16(logf)
    tril_b = tril.astype(BF16)
    same_b = same.astype(BF16)
    b = _dot(tril_b, l_hi) + _dot(tril_b, l_lo)
    bl = _dot(same_b, l_hi) + _dot(same_b, l_lo)
    qe = (q * jnp.exp(b)).astype(BF16)
    ke = (k * jnp.exp(-b)).astype(BF16)
    kd = k * jnp.exp(bl - b)
    scores = jnp.where(tril, _nt(qe, ke), 0.0)
    q_state = qe
    if ns > 1:
        earlier = ((ci >> shift) < (ri >> shift))
        later_b = ((ci >> shift) > (ri >> shift)).astype(BF16)
        earlier_b = earlier.astype(BF16)
        before = _dot(earlier_b, l_hi) + _dot(earlier_b, l_lo)
        after = _dot(later_b, l_hi) + _dot(later_b, l_lo)
        scores = scores + jnp.where(earlier, _nt(qe, kd.astype(BF16)), 0.0)
        q_state = (q * jnp.exp(b + before)).astype(BF16)
        kd = kd * jnp.exp(after)
    o = _dot(scores.astype(BF16), v)

    ones = jnp.ones((rows, LANES), BF16)
    rowseq = lax.broadcasted_iota(I32, (rows, 1), 0) >> shift
    for n in range(nb):
        s_old = s_ref[n]
        o_n = _dot(q_state, s_old.astype(BF16))
        if nb > 1:
            mine = rowseq == n
            o = o + jnp.where(mine, o_n, 0.0)
            kd_n = jnp.where(mine, kd, 0.0).astype(BF16)
            lh_n = jnp.where(mine, l_hi, jnp.zeros_like(l_hi))
            ll_n = jnp.where(mine, l_lo, jnp.zeros_like(l_lo))
        else:
            o = o + o_n
            kd_n = kd.astype(BF16)
            lh_n, ll_n = l_hi, l_lo
        decay = jnp.exp(_tn(lh_n, ones) + _tn(ll_n, ones))
        decay = jnp.concatenate([decay] * (dv // LANES), axis=1)
        s_ref[n] = s_old * decay + _tn(kd_n, v)

    o = _rms(o, ng_ref[...])
    r = r_ref[...].astype(F32)
    o_ref[...] = (o * (r * _sigmoid(r))).astype(o_ref.dtype)

    @pl.when(c == pl.num_programs(2) - 1)
    def _():
        sout_ref[...] = s_ref[...]


def gla_group(proj, gl, w_g2, b_g2, norm_g, layer, *, row0, batch, seq, nb, dk_total, dv_total,
              s0=None, name="gla"):
    H = GLA_HEADS
    dkh, dvh = dk_total // H, dv_total // H
    chunk = math.gcd(seq, GLA_CHUNK)
    ns = 2 if (nb == 1 and seq % (2 * chunk) == 0) else 1
    nchunk = seq // (ns * chunk)
    rows = nb * ns * chunk
    assert chunk & (chunk - 1) == 0 and batch % nb == 0 and row0 % rows == 0
    assert nb == 1 or nchunk == 1
    assert (2 * dk_total) % dvh == 0 and rows % 16 == 0
    rb0 = row0 // rows
    voff = 2 * dk_total // dvh
    roff = voff + H

    def rowblk(bb, c):
        return rb0 + bb * nchunk + c

    in_specs = [
        pl.BlockSpec((rows, dkh), lambda bb, h, c: (rowblk(bb, c), h)),
        pl.BlockSpec((rows, dkh), lambda bb, h, c: (rowblk(bb, c), H + h)),
        pl.BlockSpec((rows, dvh), lambda bb, h, c: (rowblk(bb, c), voff + h)),
        pl.BlockSpec((rows, dvh), lambda bb, h, c: (rowblk(bb, c), roff + h)),
        pl.BlockSpec((rows, LANES), lambda bb, h, c: (rowblk(bb, c), 0)),
        pl.BlockSpec((LANES, dkh), lambda bb, h, c: (0, h)),
        pl.BlockSpec((None, 1, dkh), lambda bb, h, c: (layer, 0, h)),
        pl.BlockSpec((None, 1, dvh), lambda bb, h, c: (layer, 0, 0)),
    ]
    args = [proj, proj, proj, proj, gl, w_g2, b_g2, norm_g]
    if s0 is not None:
        in_specs.append(pl.BlockSpec((None, nb, None, dkh, dvh), lambda bb, h, c: (layer, bb, h, 0, 0)))
        args.append(s0)
    kern = functools.partial(_gla_kernel, nb=nb, ns=ns, chunk=chunk, has_s0=s0 is not None,
                             scale=float(dkh) ** -0.5)
    return pl.pallas_call(
        kern,
        grid=(batch // nb, H, nchunk),
        in_specs=in_specs,
        out_specs=[pl.BlockSpec((rows, dvh), lambda bb, h, c: (bb * nchunk + c, h)),
                   pl.BlockSpec((nb, None, dkh, dvh), lambda bb, h, c: (bb, h, 0, 0))],
        out_shape=[jax.ShapeDtypeStruct((batch * seq, dv_total), BF16),
                   jax.ShapeDtypeStruct((batch, H, dkh, dvh), F32)],
        scratch_shapes=[pltpu.VMEM((nb, dkh, dvh), F32)],
        compiler_params=_cparams(3),
        name=name,
    )(*args)


CONV_HALO = CONV_WIDTH - 1
CONV_PAD = -(-CONV_HALO // SUBLANES) * SUBLANES
CONV_SHIFT_ROWS = (CONV_PAD // SUBLANES - 1) * SUBLANES


def _conv_kernel(*refs, nb, tl, has_cache):
    it = iter(refs)
    u_ref, wdw_ref, bdw_ref, lng_ref, lnb_ref = [next(it) for _ in range(5)]
    cache_ref = next(it) if has_cache else None
    z_ref, cout_ref, win_ref, sh_ref, acc_ref = [next(it) for _ in range(5)]

    t = pl.program_id(1)
    ch = u_ref.shape[1]
    first = CONV_PAD - CONV_HALO

    for n in range(nb):
        @pl.when(t == 0)
        def _():
            win_ref[n, 0:CONV_PAD, :] = jnp.zeros((CONV_PAD, ch), F32)
            if has_cache:
                win_ref[n, first:CONV_PAD, :] = cache_ref[n]

        @pl.when(t > 0)
        def _():
            win_ref[n, 0:CONV_PAD, :] = win_ref[n, tl:tl + CONV_PAD, :]

        win_ref[n, CONV_PAD:CONV_PAD + tl, :] = u_ref[n * tl:(n + 1) * tl, :]

        for p in range(1, SUBLANES):
            sh_ref[p - 1] = win_ref[n, p:p + tl + CONV_SHIFT_ROWS, :]

        for cb in range(ch // LANES):
            lanes = slice(cb * LANES, (cb + 1) * LANES)
            acc = jnp.zeros((tl, LANES), F32) + bdw_ref[:, lanes]
            for w in range(CONV_WIDTH):
                off = first + w
                a, p = off // SUBLANES * SUBLANES, off % SUBLANES
                rows = win_ref[n, a:a + tl, lanes] if p == 0 else sh_ref[p - 1, a:a + tl, lanes]
                acc = acc + rows * wdw_ref[w:w + 1, lanes]
            acc_ref[n * tl:(n + 1) * tl, lanes] = acc

        @pl.when(t == pl.num_programs(1) - 1)
        def _():
            cout_ref[n] = win_ref[n, CONV_PAD + tl - CONV_HALO:CONV_PAD + tl, :]

    zc = acc_ref[...]
    mu = jnp.mean(zc, axis=-1, keepdims=True)
    d = zc - mu
    var = jnp.mean(d * d, axis=-1, keepdims=True)
    y = d * lax.rsqrt(var + EPS) * lng_ref[...] + lnb_ref[...]
    z_ref[...] = (y * _sigmoid(y)).astype(z_ref.dtype)


def conv_group(u, w_dw, b_dw, ln_g, ln_b, layer, *, row0, batch, seq, nb, tl, cache=None,
               name="conv"):
    ch = u.shape[1]
    nt = seq // tl
    rows = nb * tl
    assert seq % tl == 0 and batch % nb == 0 and row0 % rows == 0 and rows % 16 == 0
    assert (nb == 1 or nt == 1) and tl % SUBLANES == 0
    rb0 = row0 // rows

    in_specs = [
        pl.BlockSpec((rows, ch), lambda bb, t: (rb0 + bb * nt + t, 0)),
        pl.BlockSpec((None, CONV_WIDTH, ch), lambda bb, t: (layer, 0, 0)),
        pl.BlockSpec((None, 1, ch), lambda bb, t: (layer, 0, 0)),
        pl.BlockSpec((None, 1, ch), lambda bb, t: (layer, 0, 0)),
        pl.BlockSpec((None, 1, ch), lambda bb, t: (layer, 0, 0)),
    ]
    args = [u, w_dw, b_dw, ln_g, ln_b]
    if cache is not None:
        in_specs.append(pl.BlockSpec((None, nb, CONV_HALO, ch), lambda bb, t: (layer, bb, 0, 0)))
        args.append(cache)
    kern = functools.partial(_conv_kernel, nb=nb, tl=tl, has_cache=cache is not None)
    return pl.pallas_call(
        kern,
        grid=(batch // nb, nt),
        in_specs=in_specs,
        out_specs=[pl.BlockSpec((rows, ch), lambda bb, t: (bb * nt + t, 0)),
                   pl.BlockSpec((nb, CONV_HALO, ch), lambda bb, t: (bb, 0, 0))],
        out_shape=[jax.ShapeDtypeStruct((batch * seq, ch), BF16),
                   jax.ShapeDtypeStruct((batch, CONV_HALO, ch), F32)],
        scratch_shapes=[pltpu.VMEM((nb, CONV_PAD + tl, ch), F32),
                        pltpu.VMEM((SUBLANES - 1, tl + CONV_SHIFT_ROWS, ch), F32),
                        pltpu.VMEM((rows, ch), F32)],
        compiler_params=_cparams(2),
        name=name,
    )(*args)


def _router_kernel(x_ref, g_ref, wr_ref, br_ref, hp_ref, ids_ref, gates_ref):
    h = _rms(x_ref[...], g_ref[...])
    hp_ref[...] = _pack_bf16_pairs(h)

    h_hi, h_lo = _split_bf16(h)
    w_hi, w_lo = _split_bf16(wr_ref[...])
    logits = _dot(h_hi, w_hi) + _dot(h_lo, w_hi) + _dot(h_hi, w_lo) + br_ref[...]

    lane = lax.broadcasted_iota(I32, logits.shape, 1).astype(F32)
    neg = jnp.float32(-jnp.inf)
    big = jnp.float32(4 * LANES)
    is_grp = lane < MOE_GROUPS
    gl = jnp.where(is_grp, logits, neg)
    gmax = jnp.max(gl, axis=-1, keepdims=True)
    g_sel = jnp.min(jnp.where(is_grp & (gl == gmax), lane, big), axis=-1, keepdims=True)
    p_g = 1.0 / jnp.sum(jnp.where(is_grp, jnp.exp(gl - gmax), 0.0), axis=-1, keepdims=True)

    lo = MOE_GROUPS + g_sel * MOE_PER_GROUP
    in_grp = (lane >= lo) & (lane < lo + MOE_PER_GROUP)
    el = jnp.where(in_grp, logits, neg)
    v1 = jnp.max(el, axis=-1, keepdims=True)
    j1 = jnp.min(jnp.where(in_grp & (el == v1), lane, big), axis=-1, keepdims=True)
    el2 = jnp.where(lane == j1, neg, el)
    v2 = jnp.max(el2, axis=-1, keepdims=True)
    j2 = jnp.min(jnp.where(in_grp & (lane != j1) & (el2 == v2), lane, big), axis=-1, keepdims=True)
    e2 = jnp.exp(v2 - v1)
    w1 = p_g / (1.0 + e2)
    w2 = p_g * e2 / (1.0 + e2)

    first = lane == 0.0
    ids_ref[...] = jnp.where(first, j1, j2).astype(I32) - MOE_GROUPS
    gates_ref[...] = jnp.where(first, w1, w2)


def router(x, g, w_r, b_r, layer, tm=256):
    T, D = x.shape
    tm = math.gcd(T, tm)
    return pl.pallas_call(
        _router_kernel,
        grid=(T // tm,),
        in_specs=[pl.BlockSpec((tm, D), lambda i: (i, 0)),
                  pl.BlockSpec((None, 1, D), lambda i: (layer, 0, 0)),
                  pl.BlockSpec((D, LANES), lambda i: (0, 0)),
                  pl.BlockSpec((1, LANES), lambda i: (0, 0))],
        out_specs=[pl.BlockSpec((tm, D // 2), lambda i: (i, 0)),
                   pl.BlockSpec((tm, LANES), lambda i: (i, 0)),
                   pl.BlockSpec((tm, LANES), lambda i: (i, 0))],
        out_shape=[jax.ShapeDtypeStruct((T, D // 2), U32),
                   jax.ShapeDtypeStruct((T, LANES), I32),
                   jax.ShapeDtypeStruct((T, LANES), F32)],
        compiler_params=_cparams(1),
        name="router",
    )(x, g, w_r, b_r)


def dispatch_tables(ids, n_experts):
    T, K = ids.shape
    A = T * K
    nc_max = A // EXPERT_ROWS + n_experts + 1
    flat_e = ids.reshape(-1)
    order = jnp.argsort(flat_e).astype(I32)
    counts = jnp.sum((flat_e[:, None] == jnp.arange(n_experts, dtype=I32)[None, :]).astype(I32), axis=0)
    seg_start = jnp.cumsum(counts) - counts
    nch = (counts + EXPERT_ROWS - 1) // EXPERT_ROWS
    ch_end = jnp.cumsum(nch)
    ch_start = ch_end - nch
    n_used = ch_end[-1]
    cidx = jnp.arange(nc_max, dtype=I32)
    c_eff = jnp.minimum(cidx, n_used - 1)
    chunk_e = jnp.minimum(jnp.sum((ch_end[None, :] <= c_eff[:, None]).astype(I32), axis=1), n_experts - 1)
    sel = (chunk_e[:, None] == jnp.arange(n_experts, dtype=I32)[None, :]).astype(I32)
    k_in_e = c_eff - jnp.sum(sel * ch_start[None, :], axis=1)
    cnt_e = jnp.sum(sel * counts[None, :], axis=1)
    nrows = jnp.where(cidx < n_used, jnp.clip(cnt_e - k_in_e * EXPERT_ROWS, 0, EXPERT_ROWS), 0)
    base = jnp.sum(sel * seg_start[None, :], axis=1) + k_in_e * EXPERT_ROWS
    return (chunk_e.astype(I32), nrows.astype(I32), base.astype(I32),
            n_used.reshape(1).astype(I32), order)


def _issue_rows(n_real, real, mixed):
    @pl.when(n_real == EXPERT_SUB)
    def _():
        lax.fori_loop(0, EXPERT_SUB, real, 0, unroll=DMA_ISSUE_UNROLL)

    @pl.when(n_real < EXPERT_SUB)
    def _():
        lax.fori_loop(0, EXPERT_SUB, mixed, 0, unroll=DMA_ISSUE_UNROLL)


def _expert_kernel(ce_ref, nrows_ref, base_ref, nused_ref, order_ref, hp_hbm, wg_ref, wu_ref,
                   wd_ref, yt_hbm, xraw_ref, x_ref, y_ref, ypk_ref, gsem, ssem, *, n_tok, n_assign,
                   nft):
    c = pl.program_id(0)
    s = pl.program_id(1)
    nused = nused_ref[0]
    nsub = (nrows_ref[c] + EXPERT_SUB - 1) // EXPERT_SUB
    half = xraw_ref.shape[1]
    n_subs = EXPERT_ROWS // EXPERT_SUB
    slot = c % 2

    def sub_rows(cc, sub):
        return (base_ref[cc] + sub * EXPERT_SUB,
                jnp.clip(nrows_ref[cc] - sub * EXPERT_SUB, 0, EXPERT_SUB))

    def issue_gather(cc, sub):
        @pl.when(sub * EXPERT_SUB < nrows_ref[cc])
        def _():
            first, n_real = sub_rows(cc, sub)
            tok0 = order_ref[base_ref[cc]] >> 1

            def dst(r):
                return xraw_ref.at[pl.ds(sub * EXPERT_SUB + r, 1), :]

            def real(r, carry):
                tok = order_ref[first + r] >> 1
                pltpu.make_async_copy(hp_hbm.at[pl.ds(tok, 1), :], dst(r), gsem.at[sub]).start()
                return carry

            def mixed(r, carry):
                tok = jnp.where(r < n_real, order_ref[first + jnp.minimum(r, n_real - 1)] >> 1, tok0)
                pltpu.make_async_copy(hp_hbm.at[pl.ds(tok, 1), :], dst(r), gsem.at[sub]).start()
                return carry

            _issue_rows(n_real, real, mixed)

    def issue_scatter(cc, sl, sub):
        @pl.when(sub * EXPERT_SUB < nrows_ref[cc])
        def _():
            first, n_real = sub_rows(cc, sub)
            dump = n_assign + cc * EXPERT_SUB

            def src(r):
                return ypk_ref.at[sl, pl.ds(sub * EXPERT_SUB + r, 1), :]

            def real(r, carry):
                a = order_ref[first + r]
                row = (a & 1) * n_tok + (a >> 1)
                pltpu.make_async_copy(src(r), yt_hbm.at[pl.ds(row, 1), :], ssem.at[sl, sub]).start()
                return carry

            def mixed(r, carry):
                a = order_ref[first + jnp.minimum(r, n_real - 1)]
                row = jnp.where(r < n_real, (a & 1) * n_tok + (a >> 1), dump + r)
                pltpu.make_async_copy(src(r), yt_hbm.at[pl.ds(row, 1), :], ssem.at[sl, sub]).start()
                return carry

            _issue_rows(n_real, real, mixed)

    def wait_scatter(cc, sl):
        for sub in range(n_subs):
            @pl.when(sub * EXPERT_SUB < nrows_ref[cc])
            def _():
                rows = pl.ds(sub * EXPERT_SUB, EXPERT_SUB)
                pltpu.make_async_copy(ypk_ref.at[sl, rows, :], yt_hbm.at[rows, :], ssem.at[sl, sub]).wait()

    @pl.when((c == 0) & (s == 0))
    def _():
        for sub in range(n_subs):
            issue_gather(0, sub)

    @pl.when((s == 0) & (c < nused))
    def _():
        for sub in range(n_subs):
            @pl.when(sub < nsub)
            def _():
                rows = pl.ds(sub * EXPERT_SUB, EXPERT_SUB)
                pltpu.make_async_copy(hp_hbm.at[rows, :], xraw_ref.at[rows, :], gsem.at[sub]).wait()
                lo, hi = _unpack_bf16_pairs(xraw_ref[rows, :])
                x_ref[rows, 0:half] = lo.astype(BF16)
                x_ref[rows, half:2 * half] = hi.astype(BF16)

    for sub in range(n_subs):
        @pl.when(s == sub % nft)
        def _():
            @pl.when(c + 1 < nused)
            def _():
                issue_gather(c + 1, sub)

            @pl.when((c >= 1) & (c <= nused))
            def _():
                issue_scatter(c - 1, 1 - slot, sub)

    @pl.when(c < nused)
    def _():
        for m in range(1, n_subs + 1):
            @pl.when(nsub == m)
            def _():
                rows = slice(0, m * EXPERT_SUB)
                x = x_ref[rows, :]
                g = _dot(x, wg_ref[...].astype(BF16))
                u = _dot(x, wu_ref[...].astype(BF16))
                hcol = (g * _sigmoid(g) * u).astype(BF16)
                part = _dot(hcol, wd_ref[...].astype(BF16))

                @pl.when(s == 0)
                def _():
                    y_ref[rows, :] = part

                @pl.when(s > 0)
                def _():
                    y_ref[rows, :] = y_ref[rows, :] + part

                @pl.when(s == nft - 1)
                def _():
                    @pl.when(c >= 2)
                    def _():
                        wait_scatter(c - 2, slot)
                    for sub in range(m):
                        piece = slice(sub * EXPERT_SUB, (sub + 1) * EXPERT_SUB)
                        ypk_ref[slot, piece, :] = _pack_bf16_pairs(y_ref[piece, :])

    @pl.when((c == pl.num_programs(0) - 1) & (s == nft - 1))
    def _():
        wait_scatter(nused - 1, (nused - 1) % 2)

        @pl.when(nused >= 2)
        def _():
            wait_scatter(nused - 2, nused % 2)


def experts(hp, w_gate, w_up, w_down, layer, chunk_e, nrows, base, n_used, order):
    T = hp.shape[0]
    D, ff = w_gate.shape[2], w_gate.shape[3]
    nc = chunk_e.shape[0]
    A = order.shape[0]
    nft = ff // EXPERT_FT
    assert ff % EXPERT_FT == 0 and A == T * MOE_TOPK and MOE_TOPK == 2

    def f_eff(c, s, nu):
        return jnp.where(c < nu[0], s, nft - 1)

    def gate_map(c, s, ce, nr, ba, nu, od):
        return (layer, ce[c], 0, f_eff(c, s, nu))

    def down_map(c, s, ce, nr, ba, nu, od):
        return (layer, ce[c], f_eff(c, s, nu), 0)

    grid_spec = pltpu.PrefetchScalarGridSpec(
        num_scalar_prefetch=5,
        grid=(nc, nft),
        in_specs=[pl.BlockSpec(memory_space=pl.ANY),
                  pl.BlockSpec((None, None, D, EXPERT_FT), gate_map),
                  pl.BlockSpec((None, None, D, EXPERT_FT), gate_map),
                  pl.BlockSpec((None, None, EXPERT_FT, D), down_map)],
        out_specs=pl.BlockSpec(memory_space=pl.ANY),
        scratch_shapes=[pltpu.VMEM((EXPERT_ROWS, D // 2), U32),
                        pltpu.VMEM((EXPERT_ROWS, D), BF16),
                        pltpu.VMEM((EXPERT_ROWS, D), F32),
                        pltpu.VMEM((2, EXPERT_ROWS, D // 2), U32),
                        pltpu.SemaphoreType.DMA((EXPERT_ROWS // EXPERT_SUB,)),
                        pltpu.SemaphoreType.DMA((2, EXPERT_ROWS // EXPERT_SUB))],
    )
    return pl.pallas_call(
        functools.partial(_expert_kernel, n_tok=T, n_assign=A, nft=nft),
        grid_spec=grid_spec,
        out_shape=jax.ShapeDtypeStruct((A + nc * EXPERT_SUB, D // 2), U32),
        compiler_params=_cparams(2),
        name="experts",
    )(chunk_e, nrows, base, n_used, order, hp, w_gate, w_up, w_down)


def _combine_kernel(x_ref, y0_ref, y1_ref, gates_ref, g_ref, *out_refs, want_x, n1):
    gates = gates_ref[...]
    y0 = jnp.concatenate(_unpack_bf16_pairs(y0_ref[...]), axis=1)
    y1 = jnp.concatenate(_unpack_bf16_pairs(y1_ref[...]), axis=1)
    x = x_ref[...] + y0 * gates[:, 0:1] + y1 * gates[:, 1:2]
    it = iter(out_refs)
    if want_x:
        next(it)[...] = x
    y = _rms(x, g_ref[...])
    if n1 is None:
        o_ref = next(it)
        o_ref[...] = y.astype(o_ref.dtype)
    else:
        op_ref, os_ref = next(it), next(it)

        @pl.when(pl.program_id(0) < n1)
        def _():
            op_ref[...] = y.astype(op_ref.dtype)

        @pl.when(pl.program_id(0) >= n1)
        def _():
            os_ref[...] = y.astype(os_ref.dtype)


def combine(x, yt, gates, g, g_layer, *, want_x, norm_dtype, split=None):
    T, D = x.shape
    tb = math.gcd(COMBINE_ROWS, T if split is None else math.gcd(*split))
    nblk = T // tb
    row = lambda i: (i, 0)
    out_specs, out_shape = [], []
    if want_x:
        out_specs.append(pl.BlockSpec((tb, D), row))
        out_shape.append(jax.ShapeDtypeStruct((T, D), F32))
    n1 = None
    if split is None:
        out_specs.append(pl.BlockSpec((tb, D), row))
        out_shape.append(jax.ShapeDtypeStruct((T, D), norm_dtype))
    else:
        assert split[0] % tb == 0 and split[1] % tb == 0 and split[0] + split[1] == T
        n1 = split[0] // tb
        out_specs += list(_two_group_specs((tb, D), n1))
        out_shape += [jax.ShapeDtypeStruct((split[0], D), norm_dtype),
                      jax.ShapeDtypeStruct((split[1], D), norm_dtype)]
    return pl.pallas_call(
        functools.partial(_combine_kernel, want_x=want_x, n1=n1),
        grid=(nblk,),
        in_specs=[pl.BlockSpec((tb, D), row),
                  pl.BlockSpec((tb, D // 2), row),
                  pl.BlockSpec((tb, D // 2), lambda i: (i + nblk, 0)),
                  pl.BlockSpec((tb, LANES), row),
                  pl.BlockSpec((None, 1, D), lambda i: (g_layer, 0, 0))],
        out_specs=out_specs,
        out_shape=out_shape,
        compiler_params=_cparams(1),
        name="combine",
    )(x, yt, yt, gates, g)


def moe_layer(x, layer, norm_ffn, w_grp, b_grp, w_exp, b_exp, w_gate, w_up, w_down,
              g_next, g_next_layer, *, want_x, norm_dtype, split=None):
    n_experts = w_exp.shape[2]
    n_route = MOE_GROUPS + n_experts
    assert n_route <= LANES and n_experts == MOE_GROUPS * MOE_PER_GROUP
    w_r = jnp.pad(jnp.concatenate([w_grp[layer], w_exp[layer]], axis=1), ((0, 0), (0, LANES - n_route)))
    b_r = jnp.pad(jnp.concatenate([b_grp[layer], b_exp[layer]]), (0, LANES - n_route)).reshape(1, LANES)
    hp, ids, gates = router(x, norm_ffn, w_r, b_r, layer)
    tables = dispatch_tables(ids[:, :MOE_TOPK], n_experts)
    yt = experts(hp, w_gate, w_up, w_down, layer, *tables)
    return combine(x, yt, gates, g_next, g_next_layer, want_x=want_x, norm_dtype=norm_dtype,
                   split=split)


def kernel(x_prompt, x_sample, state_gla, cache_conv, norm_mix, norm_ffn, norm_final,
           gla_w_in, gla_w_g2, gla_b_g2, gla_norm, gla_w_o,
           conv_w_pw1, conv_b_pw1, conv_w_dw, conv_b_dw, conv_ln_g, conv_ln_b, conv_w_pw2, conv_b_pw2,
           moe_w_grp, moe_b_grp, moe_w_exp, moe_b_exp, moe_w_gate, moe_w_up, moe_w_down):
    Bp, Lp, D = x_prompt.shape
    Bs, Ls, _ = x_sample.shape
    Tp, Ts = Bp * Lp, Bs * Ls
    depth = norm_mix.shape[0]
    assert depth == 2, "layer 0 is the GLA mixer, layer 1 the convolution mixer"
    dk = gla_w_g2.shape[2]
    dv = gla_w_o.shape[1]
    rank = gla_w_g2.shape[1]
    ch = conv_w_dw.shape[2]
    n_qkvr = 2 * dk + 2 * dv
    assert rank <= LANES and n_qkvr % LANES == 0

    def vec3(a):
        return a.reshape(a.shape[0], 1, a.shape[1])

    norm_mix3, norm_ffn3, norm_final3 = vec3(norm_mix), vec3(norm_ffn), norm_final.reshape(1, 1, D)
    x_p, x_s = x_prompt.reshape(Tp, D), x_sample.reshape(Ts, D)
    tm = math.gcd(math.gcd(Tp, Ts), 1024)
    sample_nb = 16 // math.gcd(Ls, 16)

    h = rmsnorm_rows(x_p, x_s, norm_mix3, 0, tm // 2)
    w_in_t = jnp.swapaxes(gla_w_in, 1, 2)
    proj = matmul(h, w_in_t, 0, n_qkvr, tn=512, tm=tm, w_t=True, out_dtype=BF16, name="gla_proj")
    gl = matmul(h, w_in_t, 0, LANES, tn=LANES, tm=tm, w_t=True, col0=n_qkvr, n_valid=rank,
                name="gla_gate_proj")
    w_g2 = jnp.pad(gla_w_g2[0], ((0, LANES - rank), (0, 0)))
    gla_args = (proj, gl, w_g2, vec3(gla_b_g2), vec3(gla_norm), 0)
    o_p, s_prompt = gla_group(*gla_args, row0=0, batch=Bp, seq=Lp, nb=1, dk_total=dk, dv_total=dv,
                              name="gla_prompt")
    o_s, s_sample = gla_group(*gla_args, row0=Tp, batch=Bs, seq=Ls, nb=sample_nb, dk_total=dk,
                              dv_total=dv, s0=state_gla, name="gla_sample")
    tn_out = math.gcd(D, 1024)
    x = matmul(o_p, gla_w_o, 0, D, tn=tn_out, tm=tm // 2, a2=o_s, res=x_p, res2=x_s, w_buffers=1,
               name="gla_out")
    x, h = moe_layer(x, 0, norm_ffn3, moe_w_grp, moe_b_grp, moe_w_exp, moe_b_exp,
                     moe_w_gate, moe_w_up, moe_w_down, norm_mix3, 1, want_x=True, norm_dtype=BF16)

    u = matmul(h, conv_w_pw1, 0, ch, tn=256, tm=tm, bias=vec3(conv_b_pw1), glu=True, name="conv_pw1")
    conv_args = (u, conv_w_dw, vec3(conv_b_dw), vec3(conv_ln_g), vec3(conv_ln_b), 0)
    z_p, c_prompt = conv_group(*conv_args, row0=0, batch=Bp, seq=Lp, nb=1, tl=math.gcd(Lp, 64),
                               name="conv_prompt")
    z_s, c_sample = conv_group(*conv_args, row0=Tp, batch=Bs, seq=Ls, nb=sample_nb, tl=Ls,
                               cache=cache_conv, name="conv_sample")
    x = matmul(z_p, conv_w_pw2, 0, D, tn=tn_out, tm=tm // 2, a2=z_s, bias=vec3(conv_b_pw2), res=x,
               w_buffers=1, name="conv_pw2")
    y_p, y_s = moe_layer(x, 1, norm_ffn3, moe_w_grp, moe_b_grp, moe_w_exp, moe_b_exp,
                         moe_w_gate, moe_w_up, moe_w_down, norm_final3, 0, want_x=False,
                         norm_dtype=F32, split=(Tp, Ts))

    return (y_p.reshape(Bp, Lp, D), y_s.reshape(Bs, Ls, D), s_prompt[None], s_sample[None],
            c_prompt[None], c_sample[None])
```

```python
import functools
import math

import jax
import jax.numpy as jnp
from jax import lax
from jax.experimental import pallas as pl
from jax.experimental.pallas import tpu as pltpu

F32 = jnp.float32
BF16 = jnp.bfloat16
U32 = jnp.uint32
I32 = jnp.int32

EPS = 1e-6
GLA_HEADS = 4
GLA_GATE_NORM = 16.0
GLA_CHUNK = 64
CONV_WIDTH = 31
MOE_GROUPS = 8
MOE_PER_GROUP = 8
MOE_TOPK = 2

LANES = 128
SUBLANES = 8
VMEM_LIMIT = 56 * 1024 * 1024
EXPERT_ROWS = 512
EXPERT_SUB = 128
EXPERT_FT = 256
COMBINE_ROWS = 256
DMA_ISSUE_UNROLL = 8


def _cparams(n_axes, **kw):
    return pltpu.CompilerParams(
        dimension_semantics=("arbitrary",) * n_axes, vmem_limit_bytes=VMEM_LIMIT, **kw)


def _nt(a, b):
    return lax.dot_general(a, b, (((1,), (1,)), ((), ())), preferred_element_type=F32)


def _tn(a, b):
    return lax.dot_general(a, b, (((0,), (0,)), ((), ())), preferred_element_type=F32)


def _dot(a, b):
    return jnp.dot(a, b, preferred_element_type=F32)


def _split_bf16(x):
    hi = x.astype(BF16)
    lo = (x - hi.astype(F32)).astype(BF16)
    return hi, lo


def _sigmoid(x):
    return 1.0 / (1.0 + jnp.exp(-x))


def _pack_bf16_pairs(x):
    half = x.shape[1] // 2
    bits = lax.bitcast_convert_type(x.astype(BF16).astype(F32), U32)
    return (bits[:, half:] & jnp.uint32(0xFFFF0000)) | (bits[:, :half] >> 16)


def _unpack_bf16_pairs(u):
    return (lax.bitcast_convert_type(u << 16, F32),
            lax.bitcast_convert_type(u & jnp.uint32(0xFFFF0000), F32))


def _rms(x, g):
    return x * lax.rsqrt(jnp.mean(x * x, axis=-1, keepdims=True) + EPS) * g


def _two_group_specs(block, n1):
    return (pl.BlockSpec(block, lambda *ix: (jnp.minimum(ix[-1], n1 - 1), 0)),
            pl.BlockSpec(block, lambda *ix: (jnp.maximum(ix[-1] - n1, 0), 0)))


def _rmsnorm_kernel(xp_ref, xs_ref, g_ref, w_ref, o_ref, p_ref, *, n1, n_valid):
    def rows_from(x_ref):
        h = _rms(x_ref[...], g_ref[...]).astype(o_ref.dtype)
        o_ref[...] = h
        acc = _nt(h, w_ref[...].astype(BF16))
        col = lax.broadcasted_iota(I32, acc.shape, 1)
        p_ref[...] = jnp.where(col < n_valid, acc, 0.0)

    pl.when(pl.program_id(0) < n1)(lambda: rows_from(xp_ref))
    pl.when(pl.program_id(0) >= n1)(lambda: rows_from(xs_ref))


def rmsnorm_rows(x_p, x_s, g, w_t, layer, tm, *, row0, n_valid):
    D = x_p.shape[1]
    n1 = x_p.shape[0] // tm
    T = x_p.shape[0] + x_s.shape[0]
    assert row0 % LANES == 0 and n_valid <= LANES
    return pl.pallas_call(
        functools.partial(_rmsnorm_kernel, n1=n1, n_valid=n_valid),
        grid=(T // tm,),
        in_specs=[*_two_group_specs((tm, D), n1),
                  pl.BlockSpec((None, 1, D), lambda i: (layer, 0, 0)),
                  pl.BlockSpec((None, LANES, D), lambda i: (layer, row0 // LANES, 0))],
        out_specs=[pl.BlockSpec((tm, D), lambda i: (i, 0)),
                   pl.BlockSpec((tm, LANES), lambda i: (i, 0))],
        out_shape=[jax.ShapeDtypeStruct((T, D), BF16),
                   jax.ShapeDtypeStruct((T, LANES), F32)],
        compiler_params=_cparams(1),
        name="rmsnorm",
    )(x_p, x_s, g, w_t)


def _mm_kernel(*refs, glu, has_bias, n_res, n1, w_t, n_valid):
    it = iter(refs)
    a_ref = next(it)
    a2_ref = next(it) if n1 is not None else None
    w_ref = next(it)
    w2_ref = next(it) if glu else None
    b_ref = next(it) if has_bias else None
    b2_ref = next(it) if (glu and has_bias) else None
    res_ref = next(it) if n_res >= 1 else None
    res2_ref = next(it) if n_res == 2 else None
    o_ref = next(it)
    wb_ref = next(it)
    wb2_ref = next(it) if glu else None

    @pl.when(pl.program_id(1) == 0)
    def _():
        wb_ref[...] = w_ref[...].astype(BF16)
        if glu:
            wb2_ref[...] = w2_ref[...].astype(BF16)

    mm = _nt if w_t else _dot

    def rows_from(src_ref, r_ref):
        a = src_ref[...]
        acc = mm(a, wb_ref[...])
        if n_valid is not None:
            col = lax.broadcasted_iota(I32, acc.shape, 1)
            acc = jnp.where(col < n_valid, acc, 0.0)
        if has_bias:
            acc = acc + b_ref[...]
        if glu:
            gate = mm(a, wb2_ref[...])
            if has_bias:
                gate = gate + b2_ref[...]
            acc = acc * _sigmoid(gate)
        if r_ref is not None:
            acc = acc + r_ref[...]
        o_ref[...] = acc.astype(o_ref.dtype)

    if n1 is None:
        rows_from(a_ref, res_ref)
    else:
        second = res2_ref if n_res == 2 else res_ref
        pl.when(pl.program_id(1) < n1)(lambda: rows_from(a_ref, res_ref))
        pl.when(pl.program_id(1) >= n1)(lambda: rows_from(a2_ref, second))


def matmul(a, w, layer, n_out, *, tn, tm, a2=None, bias=None, res=None, res2=None, glu=False,
           w_t=False, col0=0, n_valid=None, w_buffers=2, out_dtype=F32, name="matmul"):
    M, K = a.shape
    n1 = None
    if a2 is None:
        in_specs = [pl.BlockSpec((tm, K), lambda j, i: (i, 0))]
        args = [a]
    else:
        assert M % tm == 0 and a2.shape[0] % tm == 0
        n1 = M // tm
        M = M + a2.shape[0]
        in_specs = list(_two_group_specs((tm, K), n1))
        args = [a, a2]
    assert col0 % tn == 0 and M % tm == 0
    nj, ni = n_out // tn, M // tm
    j0 = col0 // tn
    goff = j0 + n_out // tn
    if w_t:
        wblock = (None, tn, K)
        wmap = lambda off: (lambda j, i: (layer, j + off, 0))
    else:
        wblock = (None, K, tn)
        wmap = lambda off: (lambda j, i: (layer, 0, j + off))
    wmode = {} if w_buffers == 2 else {"pipeline_mode": pl.Buffered(w_buffers)}
    in_specs.append(pl.BlockSpec(wblock, wmap(j0), **wmode))
    args.append(w)
    if glu:
        in_specs.append(pl.BlockSpec(wblock, wmap(goff), **wmode))
        args.append(w)
    if bias is not None:
        in_specs.append(pl.BlockSpec((None, 1, tn), lambda j, i: (layer, 0, j + j0)))
        args.append(bias)
        if glu:
            in_specs.append(pl.BlockSpec((None, 1, tn), lambda j, i: (layer, 0, j + goff)))
            args.append(bias)
    n_res = 0
    if res is not None and res2 is None:
        n_res = 1
        in_specs.append(pl.BlockSpec((tm, tn), lambda j, i: (i, j)))
        args.append(res)
    elif res is not None:
        n_res = 2
        in_specs.append(pl.BlockSpec((tm, tn), lambda j, i: (jnp.minimum(i, n1 - 1), j)))
        in_specs.append(pl.BlockSpec((tm, tn), lambda j, i: (jnp.maximum(i - n1, 0), j)))
        args += [res, res2]
    scratch = [pltpu.VMEM(wblock[1:], BF16)] * (2 if glu else 1)
    return pl.pallas_call(
        functools.partial(_mm_kernel, glu=glu, has_bias=bias is not None, n_res=n_res, n1=n1,
                          w_t=w_t, n_valid=n_valid),
        grid=(nj, ni),
        in_specs=in_specs,
        out_specs=pl.BlockSpec((tm, tn), lambda j, i: (i, j)),
        out_shape=jax.ShapeDtypeStruct((M, n_out), out_dtype),
        scratch_shapes=scratch,
        compiler_params=_cparams(2),
        name=name,
    )(*args)


def _gla_kernel(*refs, nb, ns, chunk, has_s0, scale):
    it = iter(refs)
    q_ref, k_ref, v_ref, r_ref, gl_ref, wg2_ref, bg2_ref, ng_ref = [next(it) for _ in range(8)]
    s0_ref = next(it) if has_s0 else None
    o_ref, sout_ref, s_ref = next(it), next(it), next(it)

    c = pl.program_id(2)
    rows = nb * ns * chunk
    dv = v_ref.shape[1]
    shift = chunk.bit_length() - 1

    @pl.when(c == 0)
    def _():
        if has_s0:
            s_ref[...] = s0_ref[...]
        else:
            s_ref[...] = jnp.zeros(s_ref.shape, F32)

    q = q_ref[...].astype(F32) * scale
    k = k_ref[...].astype(F32)
    v = v_ref[...]
    z = _dot(gl_ref[...].astype(BF16), wg2_ref[...].astype(BF16)) + bg2_ref[...]
    logf = (jnp.minimum(z, 0.0) - jnp.log1p(jnp.exp(-jnp.abs(z)))) * (1.0 / GLA_GATE_NORM)

    ri = lax.broadcasted_iota(I32, (rows, rows), 0)
    ci = lax.broadcasted_iota(I32, (rows, rows), 1)
    same = (ri >> shift) == (ci >> shift)
    tril = same & (ci <= ri)
    l_hi, l_lo = _split_bf16(logf)
    tril_b = tril.astype(BF16)
    same_b = same.astype(BF16)
    b = _dot(tril_b, l_hi) + _dot(tril_b, l_lo)
    bl = _dot(same_b, l_hi) + _dot(same_b, l_lo)
    qe = (q * jnp.exp(b)).astype(BF16)
    ke = (k * jnp.exp(-b)).astype(BF16)
    kd = k * jnp.exp(bl - b)
    scores = jnp.where(tril, _nt(qe, ke), 0.0)
    q_state = qe
    if ns > 1:
        earlier = ((ci >> shift) < (ri >> shift))
        later_b = ((ci >> shift) > (ri >> shift)).astype(BF16)
        earlier_b = earlier.astype(BF16)
        before = _dot(earlier_b, l_hi) + _dot(earlier_b, l_lo)
        after = _dot(later_b, l_hi) + _dot(later_b, l_lo)
        scores = scores + jnp.where(earlier, _nt(qe, kd.astype(BF16)), 0.0)
        q_state = (q * jnp.exp(b + before)).astype(BF16)
        kd = kd * jnp.exp(after)
    o = _dot(scores.astype(BF16), v)

    ones = jnp.ones((rows, LANES), BF16)
    rowseq = lax.broadcasted_iota(I32, (rows, 1), 0) >> shift
    for n in range(nb):
        s_old = s_ref[n]
        o_n = _dot(q_state, s_old.astype(BF16))
        if nb > 1:
            mine = rowseq == n
            o = o + jnp.where(mine, o_n, 0.0)
            kd_n = jnp.where(mine, kd, 0.0).astype(BF16)
            lh_n = jnp.where(mine, l_hi, jnp.zeros_like(l_hi))
            ll_n = jnp.where(mine, l_lo, jnp.zeros_like(l_lo))
        else:
            o = o + o_n
            kd_n = kd.astype(BF16)
            lh_n, ll_n = l_hi, l_lo
        decay = jnp.exp(_tn(lh_n, ones) + _tn(ll_n, ones))
        decay = jnp.concatenate([decay] * (dv // LANES), axis=1)
        s_ref[n] = s_old * decay + _tn(kd_n, v)

    o = _rms(o, ng_ref[...])
    r = r_ref[...].astype(F32)
    o_ref[...] = (o * (r * _sigmoid(r))).astype(o_ref.dtype)

    @pl.when(c == pl.num_programs(2) - 1)
    def _():
        sout_ref[...] = s_ref[...]


def gla_group(proj, gl, w_g2, b_g2, norm_g, layer, *, row0, batch, seq, nb, dk_total, dv_total,
              s0=None, name="gla"):
    H = GLA_HEADS
    dkh, dvh = dk_total // H, dv_total // H
    chunk = math.gcd(seq, GLA_CHUNK)
    ns = 2 if (nb == 1 and seq % (2 * chunk) == 0) else 1
    nchunk = seq // (ns * chunk)
    rows = nb * ns * chunk
    assert chunk & (chunk - 1) == 0 and batch % nb == 0 and row0 % rows == 0
    assert nb == 1 or nchunk == 1
    assert (2 * dk_total) % dvh == 0 and rows % 16 == 0
    rb0 = row0 // rows
    voff = 2 * dk_total // dvh
    roff = voff + H

    def rowblk(bb, c):
        return rb0 + bb * nchunk + c

    in_specs = [
        pl.BlockSpec((rows, dkh), lambda bb, h, c: (rowblk(bb, c), h)),
        pl.BlockSpec((rows, dkh), lambda bb, h, c: (rowblk(bb, c), H + h)),
        pl.BlockSpec((rows, dvh), lambda bb, h, c: (rowblk(bb, c), voff + h)),
        pl.BlockSpec((rows, dvh), lambda bb, h, c: (rowblk(bb, c), roff + h)),
        pl.BlockSpec((rows, LANES), lambda bb, h, c: (rowblk(bb, c), 0)),
        pl.BlockSpec((LANES, dkh), lambda bb, h, c: (0, h)),
        pl.BlockSpec((None, 1, dkh), lambda bb, h, c: (layer, 0, h)),
        pl.BlockSpec((None, 1, dvh), lambda bb, h, c: (layer, 0, 0)),
    ]
    args = [proj, proj, proj, proj, gl, w_g2, b_g2, norm_g]
    if s0 is not None:
        in_specs.append(pl.BlockSpec((None, nb, None, dkh, dvh), lambda bb, h, c: (layer, bb, h, 0, 0)))
        args.append(s0)
    kern = functools.partial(_gla_kernel, nb=nb, ns=ns, chunk=chunk, has_s0=s0 is not None,
                             scale=float(dkh) ** -0.5)
    return pl.pallas_call(
        kern,
        grid=(batch // nb, H, nchunk),
        in_specs=in_specs,
        out_specs=[pl.BlockSpec((rows, dvh), lambda bb, h, c: (bb * nchunk + c, h)),
                   pl.BlockSpec((nb, None, dkh, dvh), lambda bb, h, c: (bb, h, 0, 0))],
        out_shape=[jax.ShapeDtypeStruct((batch * seq, dv_total), BF16),
                   jax.ShapeDtypeStruct((batch, H, dkh, dvh), F32)],
        scratch_shapes=[pltpu.VMEM((nb, dkh, dvh), F32)],
        compiler_params=_cparams(3),
        name=name,
    )(*args)


CONV_HALO = CONV_WIDTH - 1
CONV_PAD = -(-CONV_HALO // SUBLANES) * SUBLANES
CONV_SHIFT_ROWS = (CONV_PAD // SUBLANES - 1) * SUBLANES


def _conv_kernel(*refs, nb, tl, has_cache):
    it = iter(refs)
    u_ref, wdw_ref, bdw_ref, lng_ref, lnb_ref = [next(it) for _ in range(5)]
    cache_ref = next(it) if has_cache else None
    z_ref, cout_ref, win_ref, sh_ref, acc_ref = [next(it) for _ in range(5)]

    t = pl.program_id(1)
    ch = u_ref.shape[1]
    first = CONV_PAD - CONV_HALO

    for n in range(nb):
        @pl.when(t == 0)
        def _():
            win_ref[n, 0:CONV_PAD, :] = jnp.zeros((CONV_PAD, ch), F32)
            if has_cache:
                win_ref[n, first:CONV_PAD, :] = cache_ref[n]

        @pl.when(t > 0)
        def _():
            win_ref[n, 0:CONV_PAD, :] = win_ref[n, tl:tl + CONV_PAD, :]

        win_ref[n, CONV_PAD:CONV_PAD + tl, :] = u_ref[n * tl:(n + 1) * tl, :]

        for p in range(1, SUBLANES):
            sh_ref[p - 1] = win_ref[n, p:p + tl + CONV_SHIFT_ROWS, :]

        for cb in range(ch // LANES):
            lanes = slice(cb * LANES, (cb + 1) * LANES)
            acc = jnp.zeros((tl, LANES), F32) + bdw_ref[:, lanes]
            for w in range(CONV_WIDTH):
                off = first + w
                a, p = off // SUBLANES * SUBLANES, off % SUBLANES
                rows = win_ref[n, a:a + tl, lanes] if p == 0 else sh_ref[p - 1, a:a + tl, lanes]
                acc = acc + rows * wdw_ref[w:w + 1, lanes]
            acc_ref[n * tl:(n + 1) * tl, lanes] = acc

        @pl.when(t == pl.num_programs(1) - 1)
        def _():
            cout_ref[n] = win_ref[n, CONV_PAD + tl - CONV_HALO:CONV_PAD + tl, :]

    zc = acc_ref[...]
    mu = jnp.mean(zc, axis=-1, keepdims=True)
    d = zc - mu
    var = jnp.mean(d * d, axis=-1, keepdims=True)
    y = d * lax.rsqrt(var + EPS) * lng_ref[...] + lnb_ref[...]
    z_ref[...] = (y * _sigmoid(y)).astype(z_ref.dtype)


def conv_group(u, w_dw, b_dw, ln_g, ln_b, layer, *, row0, batch, seq, nb, tl, cache=None,
               name="conv"):
    ch = u.shape[1]
    nt = seq // tl
    rows = nb * tl
    assert seq % tl == 0 and batch % nb == 0 and row0 % rows == 0 and rows % 16 == 0
    assert (nb == 1 or nt == 1) and tl % SUBLANES == 0
    rb0 = row0 // rows

    in_specs = [
        pl.BlockSpec((rows, ch), lambda bb, t: (rb0 + bb * nt + t, 0)),
        pl.BlockSpec((None, CONV_WIDTH, ch), lambda bb, t: (layer, 0, 0)),
        pl.BlockSpec((None, 1, ch), lambda bb, t: (layer, 0, 0)),
        pl.BlockSpec((None, 1, ch), lambda bb, t: (layer, 0, 0)),
        pl.BlockSpec((None, 1, ch), lambda bb, t: (layer, 0, 0)),
    ]
    args = [u, w_dw, b_dw, ln_g, ln_b]
    if cache is not None:
        in_specs.append(pl.BlockSpec((None, nb, CONV_HALO, ch), lambda bb, t: (layer, bb, 0, 0)))
        args.append(cache)
    kern = functools.partial(_conv_kernel, nb=nb, tl=tl, has_cache=cache is not None)
    return pl.pallas_call(
        kern,
        grid=(batch // nb, nt),
        in_specs=in_specs,
        out_specs=[pl.BlockSpec((rows, ch), lambda bb, t: (bb * nt + t, 0)),
                   pl.BlockSpec((nb, CONV_HALO, ch), lambda bb, t: (bb, 0, 0))],
        out_shape=[jax.ShapeDtypeStruct((batch * seq, ch), BF16),
                   jax.ShapeDtypeStruct((batch, CONV_HALO, ch), F32)],
        scratch_shapes=[pltpu.VMEM((nb, CONV_PAD + tl, ch), F32),
                        pltpu.VMEM((SUBLANES - 1, tl + CONV_SHIFT_ROWS, ch), F32),
                        pltpu.VMEM((rows, ch), F32)],
        compiler_params=_cparams(2),
        name=name,
    )(*args)


def _router_kernel(x_ref, g_ref, wr_ref, br_ref, hp_ref, ids_ref, gates_ref):
    h = _rms(x_ref[...], g_ref[...])
    hp_ref[...] = _pack_bf16_pairs(h)

    h_hi, h_lo = _split_bf16(h)
    w_hi, w_lo = _split_bf16(wr_ref[...])
    logits = _dot(h_hi, w_hi) + _dot(h_lo, w_hi) + _dot(h_hi, w_lo) + br_ref[...]

    lane = lax.broadcasted_iota(I32, logits.shape, 1).astype(F32)
    neg = jnp.float32(-jnp.inf)
    big = jnp.float32(4 * LANES)
    is_grp = lane < MOE_GROUPS
    gl = jnp.where(is_grp, logits, neg)
    gmax = jnp.max(gl, axis=-1, keepdims=True)
    g_sel = jnp.min(jnp.where(is_grp & (gl == gmax), lane, big), axis=-1, keepdims=True)
    p_g = 1.0 / jnp.sum(jnp.where(is_grp, jnp.exp(gl - gmax), 0.0), axis=-1, keepdims=True)

    lo = MOE_GROUPS + g_sel * MOE_PER_GROUP
    in_grp = (lane >= lo) & (lane < lo + MOE_PER_GROUP)
    el = jnp.where(in_grp, logits, neg)
    v1 = jnp.max(el, axis=-1, keepdims=True)
    j1 = jnp.min(jnp.where(in_grp & (el == v1), lane, big), axis=-1, keepdims=True)
    el2 = jnp.where(lane == j1, neg, el)
    v2 = jnp.max(el2, axis=-1, keepdims=True)
    j2 = jnp.min(jnp.where(in_grp & (lane != j1) & (el2 == v2), lane, big), axis=-1, keepdims=True)
    e2 = jnp.exp(v2 - v1)
    w1 = p_g / (1.0 + e2)
    w2 = p_g * e2 / (1.0 + e2)

    first = lane == 0.0
    ids_ref[...] = jnp.where(first, j1, j2).astype(I32) - MOE_GROUPS
    gates_ref[...] = jnp.where(first, w1, w2)


def router(x, g, w_r, b_r, layer, tm=256):
    T, D = x.shape
    tm = math.gcd(T, tm)
    return pl.pallas_call(
        _router_kernel,
        grid=(T // tm,),
        in_specs=[pl.BlockSpec((tm, D), lambda i: (i, 0)),
                  pl.BlockSpec((None, 1, D), lambda i: (layer, 0, 0)),
                  pl.BlockSpec((D, LANES), lambda i: (0, 0)),
                  pl.BlockSpec((1, LANES), lambda i: (0, 0))],
        out_specs=[pl.BlockSpec((tm, D // 2), lambda i: (i, 0)),
                   pl.BlockSpec((tm, LANES), lambda i: (i, 0)),
                   pl.BlockSpec((tm, LANES), lambda i: (i, 0))],
        out_shape=[jax.ShapeDtypeStruct((T, D // 2), U32),
                   jax.ShapeDtypeStruct((T, LANES), I32),
                   jax.ShapeDtypeStruct((T, LANES), F32)],
        compiler_params=_cparams(1),
        name="router",
    )(x, g, w_r, b_r)


def dispatch_tables(ids, n_experts):
    T, K = ids.shape
    A = T * K
    nc_max = A // EXPERT_ROWS + n_experts + 1
    flat_e = ids.reshape(-1)
    order = jnp.argsort(flat_e).astype(I32)
    counts = jnp.sum((flat_e[:, None] == jnp.arange(n_experts, dtype=I32)[None, :]).astype(I32), axis=0)
    seg_start = jnp.cumsum(counts) - counts
    nch = (counts + EXPERT_ROWS - 1) // EXPERT_ROWS
    ch_end = jnp.cumsum(nch)
    ch_start = ch_end - nch
    n_used = ch_end[-1]
    cidx = jnp.arange(nc_max, dtype=I32)
    c_eff = jnp.minimum(cidx, n_used - 1)
    chunk_e = jnp.minimum(jnp.sum((ch_end[None, :] <= c_eff[:, None]).astype(I32), axis=1), n_experts - 1)
    sel = (chunk_e[:, None] == jnp.arange(n_experts, dtype=I32)[None, :]).astype(I32)
    k_in_e = c_eff - jnp.sum(sel * ch_start[None, :], axis=1)
    cnt_e = jnp.sum(sel * counts[None, :], axis=1)
    nrows = jnp.where(cidx < n_used, jnp.clip(cnt_e - k_in_e * EXPERT_ROWS, 0, EXPERT_ROWS), 0)
    base = jnp.sum(sel * seg_start[None, :], axis=1) + k_in_e * EXPERT_ROWS
    return (chunk_e.astype(I32), nrows.astype(I32), base.astype(I32),
            n_used.reshape(1).astype(I32), order)


def _issue_rows(n_real, real, mixed):
    @pl.when(n_real == EXPERT_SUB)
    def _():
        lax.fori_loop(0, EXPERT_SUB, real, 0, unroll=DMA_ISSUE_UNROLL)

    @pl.when(n_real < EXPERT_SUB)
    def _():
        lax.fori_loop(0, EXPERT_SUB, mixed, 0, unroll=DMA_ISSUE_UNROLL)


def _expert_kernel(ce_ref, nrows_ref, base_ref, nused_ref, order_ref, hp_hbm, wg_ref, wu_ref,
                   wd_ref, yt_hbm, xraw_ref, x_ref, y_ref, ypk_ref, gsem, ssem, *, n_tok, n_assign,
                   nft):
    c = pl.program_id(0)
    s = pl.program_id(1)
    nused = nused_ref[0]
    nsub = (nrows_ref[c] + EXPERT_SUB - 1) // EXPERT_SUB
    half = xraw_ref.shape[1]
    n_subs = EXPERT_ROWS // EXPERT_SUB
    slot = c % 2

    def sub_rows(cc, sub):
        return (base_ref[cc] + sub * EXPERT_SUB,
                jnp.clip(nrows_ref[cc] - sub * EXPERT_SUB, 0, EXPERT_SUB))

    def issue_gather(cc, sub):
        @pl.when(sub * EXPERT_SUB < nrows_ref[cc])
        def _():
            first, n_real = sub_rows(cc, sub)
            tok0 = order_ref[base_ref[cc]] >> 1

            def dst(r):
                return xraw_ref.at[pl.ds(sub * EXPERT_SUB + r, 1), :]

            def real(r, carry):
                tok = order_ref[first + r] >> 1
                pltpu.make_async_copy(hp_hbm.at[pl.ds(tok, 1), :], dst(r), gsem.at[sub]).start()
                return carry

            def mixed(r, carry):
                tok = jnp.where(r < n_real, order_ref[first + jnp.minimum(r, n_real - 1)] >> 1, tok0)
                pltpu.make_async_copy(hp_hbm.at[pl.ds(tok, 1), :], dst(r), gsem.at[sub]).start()
                return carry

            _issue_rows(n_real, real, mixed)

    def issue_scatter(cc, sl, sub):
        @pl.when(sub * EXPERT_SUB < nrows_ref[cc])
        def _():
            first, n_real = sub_rows(cc, sub)
            dump = n_assign + cc * EXPERT_SUB

            def src(r):
                return ypk_ref.at[sl, pl.ds(sub * EXPERT_SUB + r, 1), :]

            def real(r, carry):
                a = order_ref[first + r]
                row = (a & 1) * n_tok + (a >> 1)
                pltpu.make_async_copy(src(r), yt_hbm.at[pl.ds(row, 1), :], ssem.at[sl, sub]).start()
                return carry

            def mixed(r, carry):
                a = order_ref[first + jnp.minimum(r, n_real - 1)]
                row = jnp.where(r < n_real, (a & 1) * n_tok + (a >> 1), dump + r)
                pltpu.make_async_copy(src(r), yt_hbm.at[pl.ds(row, 1), :], ssem.at[sl, sub]).start()
                return carry

            _issue_rows(n_real, real, mixed)

    def wait_scatter(cc, sl):
        for sub in range(n_subs):
            @pl.when(sub * EXPERT_SUB < nrows_ref[cc])
            def _():
                rows = pl.ds(sub * EXPERT_SUB, EXPERT_SUB)
                pltpu.make_async_copy(ypk_ref.at[sl, rows, :], yt_hbm.at[rows, :], ssem.at[sl, sub]).wait()

    @pl.when((c == 0) & (s == 0))
    def _():
        for sub in range(n_subs):
            issue_gather(0, sub)

    @pl.when((s == 0) & (c < nused))
    def _():
        for sub in range(n_subs):
            @pl.when(sub < nsub)
            def _():
                rows = pl.ds(sub * EXPERT_SUB, EXPERT_SUB)
                pltpu.make_async_copy(hp_hbm.at[rows, :], xraw_ref.at[rows, :], gsem.at[sub]).wait()
                lo, hi = _unpack_bf16_pairs(xraw_ref[rows, :])
                x_ref[rows, 0:half] = lo.astype(BF16)
                x_ref[rows, half:2 * half] = hi.astype(BF16)

    for sub in range(n_subs):
        @pl.when(s == sub % nft)
        def _():
            @pl.when(c + 1 < nused)
            def _():
                issue_gather(c + 1, sub)

            @pl.when((c >= 1) & (c <= nused))
            def _():
                issue_scatter(c - 1, 1 - slot, sub)

    @pl.when(c < nused)
    def _():
        for m in range(1, n_subs + 1):
            @pl.when(nsub == m)
            def _():
                rows = slice(0, m * EXPERT_SUB)
                x = x_ref[rows, :]
                g = _dot(x, wg_ref[...].astype(BF16))
                u = _dot(x, wu_ref[...].astype(BF16))
                hcol = (g * _sigmoid(g) * u).astype(BF16)
                part = _dot(hcol, wd_ref[...].astype(BF16))

                @pl.when(s == 0)
                def _():
                    y_ref[rows, :] = part

                @pl.when(s > 0)
                def _():
                    y_ref[rows, :] = y_ref[rows, :] + part

                @pl.when(s == nft - 1)
                def _():
                    @pl.when(c >= 2)
                    def _():
                        wait_scatter(c - 2, slot)
                    for sub in range(m):
                        piece = slice(sub * EXPERT_SUB, (sub + 1) * EXPERT_SUB)
                        ypk_ref[slot, piece, :] = _pack_bf16_pairs(y_ref[piece, :])

    @pl.when((c == pl.num_programs(0) - 1) & (s == nft - 1))
    def _():
        wait_scatter(nused - 1, (nused - 1) % 2)

        @pl.when(nused >= 2)
        def _():
            wait_scatter(nused - 2, nused % 2)


def experts(hp, w_gate, w_up, w_down, layer, chunk_e, nrows, base, n_used, order):
    T = hp.shape[0]
    D, ff = w_gate.shape[2], w_gate.shape[3]
    nc = chunk_e.shape[0]
    A = order.shape[0]
    nft = ff // EXPERT_FT
    assert ff % EXPERT_FT == 0 and A == T * MOE_TOPK and MOE_TOPK == 2

    def f_eff(c, s, nu):
        return jnp.where(c < nu[0], s, nft - 1)

    def gate_map(c, s, ce, nr, ba, nu, od):
        return (layer, ce[c], 0, f_eff(c, s, nu))

    def down_map(c, s, ce, nr, ba, nu, od):
        return (layer, ce[c], f_eff(c, s, nu), 0)

    grid_spec = pltpu.PrefetchScalarGridSpec(
        num_scalar_prefetch=5,
        grid=(nc, nft),
        in_specs=[pl.BlockSpec(memory_space=pl.ANY),
                  pl.BlockSpec((None, None, D, EXPERT_FT), gate_map),
                  pl.BlockSpec((None, None, D, EXPERT_FT), gate_map),
                  pl.BlockSpec((None, None, EXPERT_FT, D), down_map)],
        out_specs=pl.BlockSpec(memory_space=pl.ANY),
        scratch_shapes=[pltpu.VMEM((EXPERT_ROWS, D // 2), U32),
                        pltpu.VMEM((EXPERT_ROWS, D), BF16),
                        pltpu.VMEM((EXPERT_ROWS, D), F32),
                        pltpu.VMEM((2, EXPERT_ROWS, D // 2), U32),
                        pltpu.SemaphoreType.DMA((EXPERT_ROWS // EXPERT_SUB,)),
                        pltpu.SemaphoreType.DMA((2, EXPERT_ROWS // EXPERT_SUB))],
    )
    return pl.pallas_call(
        functools.partial(_expert_kernel, n_tok=T, n_assign=A, nft=nft),
        grid_spec=grid_spec,
        out_shape=jax.ShapeDtypeStruct((A + nc * EXPERT_SUB, D // 2), U32),
        compiler_params=_cparams(2),
        name="experts",
    )(chunk_e, nrows, base, n_used, order, hp, w_gate, w_up, w_down)


def _combine_kernel(x_ref, y0_ref, y1_ref, gates_ref, g_ref, *out_refs, want_x, n1):
    gates = gates_ref[...]
    y0 = jnp.concatenate(_unpack_bf16_pairs(y0_ref[...]), axis=1)
    y1 = jnp.concatenate(_unpack_bf16_pairs(y1_ref[...]), axis=1)
    x = x_ref[...] + y0 * gates[:, 0:1] + y1 * gates[:, 1:2]
    it = iter(out_refs)
    if want_x:
        next(it)[...] = x
    y = _rms(x, g_ref[...])
    if n1 is None:
        o_ref = next(it)
        o_ref[...] = y.astype(o_ref.dtype)
    else:
        op_ref, os_ref = next(it), next(it)

        @pl.when(pl.program_id(0) < n1)
        def _():
            op_ref[...] = y.astype(op_ref.dtype)

        @pl.when(pl.program_id(0) >= n1)
        def _():
            os_ref[...] = y.astype(os_ref.dtype)


def combine(x, yt, gates, g, g_layer, *, want_x, norm_dtype, split=None):
    T, D = x.shape
    tb = math.gcd(COMBINE_ROWS, T if split is None else math.gcd(*split))
    nblk = T // tb
    row = lambda i: (i, 0)
    out_specs, out_shape = [], []
    if want_x:
        out_specs.append(pl.BlockSpec((tb, D), row))
        out_shape.append(jax.ShapeDtypeStruct((T, D), F32))
    n1 = None
    if split is None:
        out_specs.append(pl.BlockSpec((tb, D), row))
        out_shape.append(jax.ShapeDtypeStruct((T, D), norm_dtype))
    else:
        assert split[0] % tb == 0 and split[1] % tb == 0 and split[0] + split[1] == T
        n1 = split[0] // tb
        out_specs += list(_two_group_specs((tb, D), n1))
        out_shape += [jax.ShapeDtypeStruct((split[0], D), norm_dtype),
                      jax.ShapeDtypeStruct((split[1], D), norm_dtype)]
    return pl.pallas_call(
        functools.partial(_combine_kernel, want_x=want_x, n1=n1),
        grid=(nblk,),
        in_specs=[pl.BlockSpec((tb, D), row),
                  pl.BlockSpec((tb, D // 2), row),
                  pl.BlockSpec((tb, D // 2), lambda i: (i + nblk, 0)),
                  pl.BlockSpec((tb, LANES), row),
                  pl.BlockSpec((None, 1, D), lambda i: (g_layer, 0, 0))],
        out_specs=out_specs,
        out_shape=out_shape,
        compiler_params=_cparams(1),
        name="combine",
    )(x, yt, yt, gates, g)


def moe_layer(x, layer, norm_ffn, w_grp, b_grp, w_exp, b_exp, w_gate, w_up, w_down,
              g_next, g_next_layer, *, want_x, norm_dtype, split=None):
    n_experts = w_exp.shape[2]
    n_route = MOE_GROUPS + n_experts
    assert n_route <= LANES and n_experts == MOE_GROUPS * MOE_PER_GROUP
    w_r = jnp.pad(jnp.concatenate([w_grp[layer], w_exp[layer]], axis=1), ((0, 0), (0, LANES - n_route)))
    b_r = jnp.pad(jnp.concatenate([b_grp[layer], b_exp[layer]]), (0, LANES - n_route)).reshape(1, LANES)
    hp, ids, gates = router(x, norm_ffn, w_r, b_r, layer)
    tables = dispatch_tables(ids[:, :MOE_TOPK], n_experts)
    yt = experts(hp, w_gate, w_up, w_down, layer, *tables)
    return combine(x, yt, gates, g_next, g_next_layer, want_x=want_x, norm_dtype=norm_dtype,
                   split=split)


def kernel(x_prompt, x_sample, state_gla, cache_conv, norm_mix, norm_ffn, norm_final,
           gla_w_in, gla_w_g2, gla_b_g2, gla_norm, gla_w_o,
           conv_w_pw1, conv_b_pw1, conv_w_dw, conv_b_dw, conv_ln_g, conv_ln_b, conv_w_pw2, conv_b_pw2,
           moe_w_grp, moe_b_grp, moe_w_exp, moe_b_exp, moe_w_gate, moe_w_up, moe_w_down):
    Bp, Lp, D = x_prompt.shape
    Bs, Ls, _ = x_sample.shape
    Tp, Ts = Bp * Lp, Bs * Ls
    depth = norm_mix.shape[0]
    assert depth == 2, "layer 0 is the GLA mixer, layer 1 the convolution mixer"
    dk = gla_w_g2.shape[2]
    dv = gla_w_o.shape[1]
    rank = gla_w_g2.shape[1]
    ch = conv_w_dw.shape[2]
    n_qkvr = 2 * dk + 2 * dv
    assert rank <= LANES and n_qkvr % LANES == 0

    def vec3(a):
        return a.reshape(a.shape[0], 1, a.shape[1])

    norm_mix3, norm_ffn3, norm_final3 = vec3(norm_mix), vec3(norm_ffn), norm_final.reshape(1, 1, D)
    x_p, x_s = x_prompt.reshape(Tp, D), x_sample.reshape(Ts, D)
    tm = math.gcd(math.gcd(Tp, Ts), 1024)
    sample_nb = 16 // math.gcd(Ls, 16)

    w_in_t = jnp.swapaxes(gla_w_in, 1, 2)
    h, gl = rmsnorm_rows(x_p, x_s, norm_mix3, w_in_t, 0, tm // 2, row0=n_qkvr, n_valid=rank)
    proj = matmul(h, w_in_t, 0, n_qkvr, tn=512, tm=tm, w_t=True, out_dtype=BF16, name="gla_proj")
    w_g2 = jnp.pad(gla_w_g2[0], ((0, LANES - rank), (0, 0)))
    gla_args = (proj, gl, w_g2, vec3(gla_b_g2), vec3(gla_norm), 0)
    o_p, s_prompt = gla_group(*gla_args, row0=0, batch=Bp, seq=Lp, nb=1, dk_total=dk, dv_total=dv,
                              name="gla_prompt")
    o_s, s_sample = gla_group(*gla_args, row0=Tp, batch=Bs, seq=Ls, nb=sample_nb, dk_total=dk,
                              dv_total=dv, s0=state_gla, name="gla_sample")
    tn_out = math.gcd(D, 1024)
    x = matmul(o_p, gla_w_o, 0, D, tn=tn_out, tm=tm // 2, a2=o_s, res=x_p, res2=x_s, w_buffers=1,
               name="gla_out")
    x, h = moe_layer(x, 0, norm_ffn3, moe_w_grp, moe_b_grp, moe_w_exp, moe_b_exp,
                     moe_w_gate, moe_w_up, moe_w_down, norm_mix3, 1, want_x=True, norm_dtype=BF16)

    u = matmul(h, conv_w_pw1, 0, ch, tn=256, tm=tm, bias=vec3(conv_b_pw1), glu=True, name="conv_pw1")
    conv_args = (u, conv_w_dw, vec3(conv_b_dw), vec3(conv_ln_g), vec3(conv_ln_b), 0)
    z_p, c_prompt = conv_group(*conv_args, row0=0, batch=Bp, seq=Lp, nb=1, tl=math.gcd(Lp, 128),
                               name="conv_prompt")
    z_s, c_sample = conv_group(*conv_args, row0=Tp, batch=Bs, seq=Ls, nb=sample_nb, tl=Ls,
                               cache=cache_conv, name="conv_sample")
    x = matmul(z_p, conv_w_pw2, 0, D, tn=tn_out, tm=tm // 2, a2=z_s, bias=vec3(conv_b_pw2), res=x,
               w_buffers=1, name="conv_pw2")
    y_p, y_s = moe_layer(x, 1, norm_ffn3, moe_w_grp, moe_b_grp, moe_w_exp, moe_b_exp,
                         moe_w_gate, moe_w_up, moe_w_down, norm_final3, 0, want_x=False,
                         norm_dtype=F32, split=(Tp, Ts))

    return (y_p.reshape(Bp, Lp, D), y_s.reshape(Bs, Ls, D), s_prompt[None], s_sample[None],
            c_prompt[None], c_sample[None])
```
